```python
import jax, jax.numpy as jnp
from jax import lax
import numpy as np

D_MODEL = 1024
BATCH = 4
SEQ = 8192
DEPTH = 4

GRID_W = 64
CTX_LEN = 256
N_MIXERS = 3
CHUNK = 128
NORM_EPS = 1e-6
RET_HEADS = 4
RET_QK_DIM = 256
RET_V_DIM = 512
RET_QK_WIDTH = RET_HEADS * RET_QK_DIM
RET_WIDTH = RET_HEADS * RET_V_DIM
ROPE_BASE = 10000.0
GM_WIDTH = 2 * D_MODEL
GM_GROUPS = 8
RW_HEAD = 64
RW_WIDTH = D_MODEL
RW_HEADS = RW_WIDTH // RW_HEAD
RW_LORA = 64
RW_LNX_EPS = 64e-5

N_RET = len(range(0, DEPTH, N_MIXERS))
N_GM = len(range(1, DEPTH, N_MIXERS))
N_RW = len(range(2, DEPTH, N_MIXERS))

kernel_name = 'hybrid_retention_gmlp_rwkv7_prefix_dit'

F32 = jnp.float32


def rmsnorm(x, g):
    xf = x.astype(F32)
    y = xf * lax.rsqrt(jnp.mean(xf * xf, axis=-1, keepdims=True) + NORM_EPS)
    return (y * g.astype(F32)).astype(x.dtype)


def to_heads(t, n_heads):
    b, l, _ = t.shape
    return t.reshape(b, l, n_heads, -1).transpose(0, 2, 1, 3).astype(F32)


def axial_rope(t, row, col):
    dk = t.shape[-1]
    n_freq = dk // 4
    freqs = ROPE_BASE ** (-jnp.arange(n_freq, dtype=F32) / n_freq)
    ang = jnp.concatenate([row.astype(F32)[:, None] * freqs, col.astype(F32)[:, None] * freqs], axis=-1)
    cos, sin = jnp.cos(ang), jnp.sin(ang)
    t1, t2 = t[..., :dk // 2], t[..., dk // 2:]
    return jnp.concatenate([t1 * cos - t2 * sin, t1 * sin + t2 * cos], axis=-1)


def retention_chunkwise(q, k, v, log_g, state0):
    b, h, l, _ = q.shape
    n = l // CHUNK

    def blocks(t):
        return jnp.moveaxis(t.reshape(b, h, n, CHUNK, t.shape[-1]), 2, 0)

    idx = jnp.arange(CHUNK, dtype=F32)
    diff = idx[:, None] - idx[None, :]
    intra = jnp.where(diff >= 0, jnp.exp(jnp.maximum(diff, 0.0)[None] * log_g[:, None, None]), 0.0)
    q_decay = jnp.exp((idx + 1.0)[None, :] * log_g[:, None])[:, :, None]
    k_decay = jnp.exp((CHUNK - 1.0 - idx)[None, :] * log_g[:, None])[:, :, None]
    chunk_decay = jnp.exp(CHUNK * log_g)[:, None, None]

    def step(s, blk):
        qb, kb, vb = blk
        scores = jnp.einsum('bhid,bhjd->bhij', qb, kb) * intra
        o = jnp.einsum('bhij,bhjv->bhiv', scores, vb) + jnp.einsum('bhid,bhdv->bhiv', qb * q_decay, s)
        s = s * chunk_decay + jnp.einsum('bhjd,bhjv->bhdv', kb * k_decay, vb)
        return s, o

    s_final, o = lax.scan(step, state0, (blocks(q), blocks(k), blocks(v)))
    o = jnp.moveaxis(o, 0, 2).reshape(b, h, l, v.shape[-1])
    return o, s_final


def retention_bidirectional(q, k, v, log_g, s_fwd, s_bwd):
    o_f, st_f = retention_chunkwise(q, k, v, log_g[0], s_fwd)
    o_b, st_b = retention_chunkwise(jnp.flip(q, 2), jnp.flip(k, 2), jnp.flip(v, 2), log_g[1], s_bwd)
    return o_f + jnp.flip(o_b, 2), st_f, st_b


def retention_project(h, w_in):
    a, bq, cq = RET_QK_WIDTH, 2 * RET_QK_WIDTH, 2 * RET_QK_WIDTH + RET_WIDTH
    q = to_heads(h @ w_in[:, :a], RET_HEADS)
    k = to_heads(h @ w_in[:, a:bq], RET_HEADS) * (RET_QK_DIM ** -0.5)
    v = to_heads(h @ w_in[:, bq:cq], RET_HEADS)
    z = h @ w_in[:, cq:]
    return q, k, v, z


def retention_output(o, z, w_out):
    o = o * lax.rsqrt(jnp.mean(o * o, axis=-1, keepdims=True) + NORM_EPS)
    b, h, l, dv = o.shape
    o = o.transpose(0, 2, 1, 3).reshape(b, l, h * dv).astype(z.dtype)
    return (o * jax.nn.silu(z)) @ w_out


def retention_mixer(hx, hc, w_in, decay_logit, w_out, need_ctx):
    l = hx.shape[1]
    t = jnp.arange(l, dtype=jnp.int32)
    row, col = t // GRID_W, t % GRID_W
    log_g = jax.nn.log_sigmoid(decay_logit.astype(F32))
    qc, kc, vc, zc = retention_project(hc, w_in)
    qx, kx, vx, zx = retention_project(hx, w_in)
    qx, kx = axial_rope(qx, row, col), axial_rope(kx, row, col)
    zero = jnp.zeros((hx.shape[0], RET_HEADS, RET_QK_DIM, RET_V_DIM), F32)
    oc, s_f, s_b = retention_bidirectional(qc, kc, vc, log_g, zero, zero)
    ox, _, _ = retention_bidirectional(qx, kx, vx, log_g, s_f, s_b)
    out_x = retention_output(ox, zx, w_out)
    out_c = retention_output(oc, zc, w_out) if need_ctx else None
    return out_x, out_c


def gmlp_chunk_mixer(h, w_in, vnorm_g, w_s, b_s, w_out):
    u = h @ w_in[:, :GM_WIDTH]
    v = (h @ w_in[:, GM_WIDTH:2 * GM_WIDTH]).astype(F32)
    z = h @ w_in[:, 2 * GM_WIDTH:]
    v = v - jnp.mean(v, axis=-1, keepdims=True)
    v = v * lax.rsqrt(jnp.mean(v * v, axis=-1, keepdims=True) + NORM_EPS) * vnorm_g.astype(F32)
    b, l, w = v.shape
    vg = v.reshape(b, l // CHUNK, CHUNK, GM_GROUPS, w // GM_GROUPS)
    mixed = jnp.einsum('gij,bnjgc->bnigc', w_s.astype(F32), vg) + b_s.astype(F32).T[:, :, None]
    y = u * mixed.reshape(b, l, w).astype(u.dtype)
    return (y * jax.nn.silu(z)) @ w_out


def token_shift_grid(h):
    l, b, d = h.shape
    rows = l // GRID_W
    g = h.reshape(rows, GRID_W, b, d)
    q = d // 4
    left = jnp.pad(g[:, :-1, :, :q], ((0, 0), (1, 0), (0, 0), (0, 0)))
    right = jnp.pad(g[:, 1:, :, q:2 * q], ((0, 0), (0, 1), (0, 0), (0, 0)))
    up = jnp.pad(g[:-1, :, :, 2 * q:3 * q], ((1, 0), (0, 0), (0, 0), (0, 0)))
    down = jnp.pad(g[1:, :, :, 3 * q:], ((0, 1), (0, 0), (0, 0), (0, 0)))
    return jnp.concatenate([left, right, up, down], axis=-1).reshape(l, b, d)


def token_shift_seq(h):
    half = h.shape[-1] // 2
    prev = jnp.pad(h[:-1, :, :half], ((1, 0), (0, 0), (0, 0)))
    nxt = jnp.pad(h[1:, :, half:], ((0, 1), (0, 0), (0, 0)))
    return jnp.concatenate([prev, nxt], axis=-1)


def rwkv_prepare(h, shifted, mu, w_rkvg, w0, w1, w2, a0, a1, a2, k_k, k_a):
    l, b, _ = h.shape
    xx = shifted - h

    def mix(p):
        return h + xx * mu[p]

    def heads(t):
        return t.astype(F32).reshape(l, b, RW_HEADS, RW_HEAD)

    r = heads(mix(0) @ w_rkvg[0])
    k = heads(mix(2) @ w_rkvg[1])
    v = heads(mix(3) @ w_rkvg[2])
    z = mix(5) @ w_rkvg[3]
    xw, xa = mix(1), mix(4)
    kk = k * k_k.astype(F32).reshape(RW_HEADS, RW_HEAD)
    kk = kk / jnp.maximum(jnp.sqrt(jnp.sum(kk * kk, axis=-1, keepdims=True)), 1e-12)
    k_a_h = k_a.astype(F32).reshape(RW_HEADS, RW_HEAD)
    dirs = []
    for d in range(2):
        w_log = -jax.nn.softplus(-(w0[d] + jnp.tanh(xw @ w1[d]) @ w2[d]).astype(F32)) - 0.5
        dec = heads(jnp.exp(-jnp.exp(w_log)))
        a = heads(jax.nn.sigmoid((a0[d] + (xa @ a1[d]) @ a2[d]).astype(F32)))
        dirs.append((dec, a, k * (1.0 + (a - 1.0) * k_a_h)))
    return r, v, kk, z, dirs


def wkv_scan(state0, r, dec, k, v, kk, a, reverse):
    def step(s, inp):
        r_t, w_t, k_t, v_t, kk_t, a_t = inp
        sa = jnp.einsum('bhvk,bhk->bhv', s, -kk_t)
        s = s * w_t[:, :, None, :] + sa[..., None] * (kk_t * a_t)[:, :, None, :] + v_t[..., None] * k_t[:, :, None, :]
        return s, jnp.einsum('bhvk,bhk->bhv', s, r_t)

    s_final, y = lax.scan(step, state0, (r, dec, k, v, kk, a), reverse=reverse)
    return y, s_final


def rwkv_bidirectional(prep, states):
    r, v, kk, _, dirs = prep
    y_f, s_f = wkv_scan(states[0], r, dirs[0][0], dirs[0][2], v, kk, dirs[0][1], False)
    y_b, s_b = wkv_scan(states[1], r, dirs[1][0], dirs[1][2], v, kk, dirs[1][1], True)
    return y_f + y_b, (s_f, s_b)


def rwkv_output(prep, y, r_k, lnx_g, lnx_b, w_out):
    r, v, _, z, dirs = prep
    l, b, h, d = y.shape
    yc = y - jnp.mean(y, axis=-1, keepdims=True)
    yn = yc * lax.rsqrt(jnp.mean(yc * yc, axis=-1, keepdims=True) + RW_LNX_EPS)
    yn = yn.reshape(l, b, h * d) * lnx_g.astype(F32) + lnx_b.astype(F32)
    rk = r_k.astype(F32)
    bonus = (jnp.sum(r * dirs[0][2] * rk, axis=-1, keepdims=True)
             + jnp.sum(r * dirs[1][2] * rk, axis=-1, keepdims=True)) * v
    o = (yn + bonus.reshape(l, b, h * d)).astype(z.dtype) * jax.nn.silu(z)
    return (o @ w_out).transpose(1, 0, 2)


def rwkv_mixer(hx, hc, mu, w_rkvg, w0, w1, w2, a0, a1, a2, k_k, k_a, r_k, lnx_g, lnx_b, w_out, need_ctx):
    hx_t = hx.transpose(1, 0, 2)
    hc_t = hc.transpose(1, 0, 2)
    prep_c = rwkv_prepare(hc_t, token_shift_seq(hc_t), mu, w_rkvg, w0, w1, w2, a0, a1, a2, k_k, k_a)
    prep_x = rwkv_prepare(hx_t, token_shift_grid(hx_t), mu, w_rkvg, w0, w1, w2, a0, a1, a2, k_k, k_a)
    zero = jnp.zeros((hx.shape[0], RW_HEADS, RW_HEAD, RW_HEAD), F32)
    y_c, s_c = rwkv_bidirectional(prep_c, (zero, zero))
    y_x, _ = rwkv_bidirectional(prep_x, s_c)
    out_x = rwkv_output(prep_x, y_x, r_k, lnx_g, lnx_b, w_out)
    out_c = rwkv_output(prep_c, y_c, r_k, lnx_g, lnx_b, w_out) if need_ctx else None
    return out_x, out_c


def setup_inputs(seed: int = 0) -> dict:
    key = jax.random.key(seed)
    ks = iter(jax.random.split(key, 48))
    D = D_MODEL

    def nrm(shape, s):
        return s * jax.random.normal(next(ks), shape, F32)

    x = nrm((BATCH, SEQ, D), 1.0)
    c = nrm((BATCH, D), 1.0)
    ctx = nrm((BATCH, CTX_LEN, D), 1.0)
    c_ctx = nrm((D,), 1.0)
    ada_w = nrm((DEPTH, D, 3 * D), 0.5 * D ** -0.5)
    ada_b = jnp.concatenate([nrm((DEPTH, 2 * D), 0.02), 1.0 + nrm((DEPTH, D), 0.02)], axis=-1)
    norm_g = 1.0 + nrm((DEPTH, D), 0.02)
    final_g = 1.0 + nrm((D,), 0.02)
    ret_w_in = nrm((N_RET, D, 2 * RET_QK_WIDTH + 2 * RET_WIDTH), D ** -0.5)
    gamma = 1.0 - 2.0 ** (-5.0 - jnp.arange(RET_HEADS, dtype=F32))
    ret_decay = (jnp.log(gamma) - jnp.log1p(-gamma)) + nrm((N_RET, 2, RET_HEADS), 0.1)
    ret_w_out = nrm((N_RET, RET_WIDTH, D), RET_WIDTH ** -0.5)
    gm_w_in = nrm((N_GM, D, 3 * GM_WIDTH), D ** -0.5)
    gm_vnorm_g = 1.0 + nrm((N_GM, GM_WIDTH), 0.02)
    gm_w_s = nrm((N_GM, GM_GROUPS, CHUNK, CHUNK), CHUNK ** -0.5)
    gm_b_s = 1.0 + nrm((N_GM, GM_GROUPS, CHUNK), 0.02)
    gm_w_out = nrm((N_GM, GM_WIDTH, D), GM_WIDTH ** -0.5)
    rw_mu = jax.random.uniform(next(ks), (N_RW, 6, D), F32)
    rw_w_rkvg = nrm((N_RW, 4, D, RW_WIDTH), D ** -0.5)
    rw_w0 = jnp.linspace(-6.0, -1.0, RW_WIDTH, dtype=F32) + nrm((N_RW, 2, RW_WIDTH), 0.1)
    rw_w1 = nrm((N_RW, 2, D, RW_LORA), D ** -0.5)
    rw_w2 = nrm((N_RW, 2, RW_LORA, RW_WIDTH), 0.1 * RW_LORA ** -0.5)
    rw_a0 = nrm((N_RW, 2, RW_WIDTH), 0.1)
    rw_a1 = nrm((N_RW, 2, D, RW_LORA), D ** -0.5)
    rw_a2 = nrm((N_RW, 2, RW_LORA, RW_WIDTH), 0.1 * RW_LORA ** -0.5)
    rw_k_k = 0.85 + nrm((N_RW, RW_WIDTH), 0.02)
    rw_k_a = 1.0 + nrm((N_RW, RW_WIDTH), 0.02)
    rw_r_k = nrm((N_RW, RW_HEADS, RW_HEAD), 0.1)
    rw_lnx_g = 1.0 + nrm((N_RW, RW_WIDTH), 0.02)
    rw_lnx_b = nrm((N_RW, RW_WIDTH), 0.02)
    rw_w_out = nrm((N_RW, RW_WIDTH, D), RW_WIDTH ** -0.5)
    return {'x': x, 'c': c, 'ctx': ctx, 'c_ctx': c_ctx,
            'ada_w': ada_w, 'ada_b': ada_b, 'norm_g': norm_g, 'final_g': final_g,
            'ret_w_in': ret_w_in, 'ret_decay': ret_decay, 'ret_w_out': ret_w_out,
            'gm_w_in': gm_w_in, 'gm_vnorm_g': gm_vnorm_g, 'gm_w_s': gm_w_s, 'gm_b_s': gm_b_s, 'gm_w_out': gm_w_out,
            'rw_mu': rw_mu, 'rw_w_rkvg': rw_w_rkvg, 'rw_w0': rw_w0, 'rw_w1': rw_w1, 'rw_w2': rw_w2,
            'rw_a0': rw_a0, 'rw_a1': rw_a1, 'rw_a2': rw_a2, 'rw_k_k': rw_k_k, 'rw_k_a': rw_k_a,
            'rw_r_k': rw_r_k, 'rw_lnx_g': rw_lnx_g, 'rw_lnx_b': rw_lnx_b, 'rw_w_out': rw_w_out}


def reference(x, c, ctx, c_ctx, ada_w, ada_b, norm_g, final_g,
              ret_w_in, ret_decay, ret_w_out,
              gm_w_in, gm_vnorm_g, gm_w_s, gm_b_s, gm_w_out,
              rw_mu, rw_w_rkvg, rw_w0, rw_w1, rw_w2, rw_a0, rw_a1, rw_a2,
              rw_k_k, rw_k_a, rw_r_k, rw_lnx_g, rw_lnx_b, rw_w_out):
    silu_c = jax.nn.silu(c)
    silu_cc = jax.nn.silu(c_ctx)
    for i in range(DEPTH):
        kind, j = i % N_MIXERS, i // N_MIXERS
        need_ctx = i < DEPTH - 1
        shift_x, scale_x, gate_x = jnp.split(silu_c @ ada_w[i] + ada_b[i], 3, axis=-1)
        shift_c, scale_c, gate_c = jnp.split(silu_cc @ ada_w[i] + ada_b[i], 3, axis=-1)
        hx = rmsnorm(x, norm_g[i]) * (1 + scale_x[:, None]) + shift_x[:, None]
        hc = rmsnorm(ctx, norm_g[i]) * (1 + scale_c) + shift_c
        if kind == 0:
            out_x, out_c = retention_mixer(hx, hc, ret_w_in[j], ret_decay[j], ret_w_out[j], need_ctx)
        elif kind == 1:
            out_x = gmlp_chunk_mixer(hx, gm_w_in[j], gm_vnorm_g[j], gm_w_s[j], gm_b_s[j], gm_w_out[j])
            out_c = gmlp_chunk_mixer(hc, gm_w_in[j], gm_vnorm_g[j], gm_w_s[j], gm_b_s[j], gm_w_out[j]) if need_ctx else None
        else:
            out_x, out_c = rwkv_mixer(hx, hc, rw_mu[j], rw_w_rkvg[j], rw_w0[j], rw_w1[j], rw_w2[j],
                                      rw_a0[j], rw_a1[j], rw_a2[j], rw_k_k[j], rw_k_a[j], rw_r_k[j],
                                      rw_lnx_g[j], rw_lnx_b[j], rw_w_out[j], need_ctx)
        x = x + gate_x[:, None] * out_x
        if need_ctx:
            ctx = ctx + gate_c * out_c
    return rmsnorm(x, final_g)
```

```python
import functools

import jax
import jax.numpy as jnp
from jax import lax
from jax.experimental import pallas as pl
from jax.experimental.pallas import tpu as pltpu

F32 = jnp.float32
BF16 = jnp.bfloat16

D = 1024
GRID_W = 64
NORM_EPS = 1e-6
N_MIXERS = 3
ROW_TILE = 256
RET_HEADS = 4
RET_DK = 256
RET_DV = 512
RET_CHUNK = 128
ROPE_BASE = 10000.0
GM_WIDTH = 2 * D
GM_GROUPS = 8
GM_CHUNK = 128
RW_HEAD = 64
RW_LORA = 64
RW_LNX_EPS = 64e-5
WKV_CHUNK = 64
WKV_LANES = 256
WKV_GROUP = WKV_LANES // RW_HEAD
VMEM_LIMIT = 56 * 1024 * 1024

_NT = (((1,), (1,)), ((), ()))
_TN = (((0,), (0,)), ((), ()))


def _params(*sem):
    return pltpu.CompilerParams(dimension_semantics=sem, vmem_limit_bytes=VMEM_LIMIT)


def _silu(x):
    return x / (1.0 + jnp.exp(-x))


def _bdot(a, b, dims=None):
    a = a.astype(BF16)
    b = b.astype(BF16)
    if dims is None:
        return jnp.dot(a, b, preferred_element_type=F32)
    return lax.dot_general(a, b, dims, preferred_element_type=F32)


def _split3(x):
    hi = x.astype(BF16)
    r1 = x - hi.astype(F32)
    mid = r1.astype(BF16)
    lo = (r1 - mid.astype(F32)).astype(BF16)
    return hi, mid, lo


def _norm_mod(x, g, mod):
    ms = jnp.mean(x * x, axis=-1, keepdims=True)
    y = x * lax.rsqrt(ms + NORM_EPS) * g
    return y * (1.0 + mod[:, D:2 * D]) + mod[:, :D]


def _mod_spec(layer, n_batch, tile_of_step):
    def index(b, s):
        t = tile_of_step(s)
        return (layer, jnp.where(t == 0, n_batch, b), 0, 0)
    return pl.BlockSpec((None, None, 1, 3 * D), index)


def _identity(s):
    return s


def _mod_kernel(c_ref, w_ref, b_ref, o_ref):
    s = _silu(c_ref[...])
    o_ref[...] = jnp.dot(s, w_ref[...], preferred_element_type=F32,
                         precision=lax.Precision.HIGHEST) + b_ref[...]


def _modulation(cvec, ada_w, ada_b):
    depth = ada_w.shape[0]
    return pl.pallas_call(
        _mod_kernel,
        grid=(depth, 3),
        in_specs=[pl.BlockSpec((8, D), lambda i, j: (0, 0)),
                  pl.BlockSpec((None, D, D), lambda i, j: (i, 0, j)),
                  pl.BlockSpec((None, 1, D), lambda i, j: (i, 0, j))],
        out_specs=pl.BlockSpec((None, 8, D), lambda i, j: (i, 0, j)),
        out_shape=jax.ShapeDtypeStruct((depth, 8, 3 * D), F32),
        compiler_params=_params("arbitrary", "arbitrary"),
    )(cvec, ada_w, ada_b.reshape(depth, 1, 3 * D))


def _proj_kernel(x_ref, mod_ref, g_ref, w_ref, cs_ref, o_ref, *, rope_cols):
    h = _norm_mod(x_ref[...], g_ref[...], mod_ref[...]).astype(BF16)
    n_out = o_ref.shape[-1]
    half = RET_DK // 2
    cos = cs_ref[:, :half]
    sin = cs_ref[:, half:]
    for c0 in range(0, rope_cols, RET_DK):
        acc = jnp.dot(h, w_ref[:, c0:c0 + RET_DK], preferred_element_type=F32)
        t1, t2 = acc[:, :half], acc[:, half:]
        o_ref[:, c0:c0 + half] = (t1 * cos - t2 * sin).astype(BF16)
        o_ref[:, c0 + half:c0 + RET_DK] = (t1 * sin + t2 * cos).astype(BF16)
    nb = 512
    for c0 in range(rope_cols, n_out, nb):
        o_ref[:, c0:c0 + nb] = jnp.dot(h, w_ref[:, c0:c0 + nb], preferred_element_type=F32).astype(BF16)


def _project(xc, mod4, g, w, cs, layer, rope_cols):
    n_batch, lc, _ = xc.shape
    n_out = w.shape[1]
    tiles = lc // ROW_TILE
    return pl.pallas_call(
        functools.partial(_proj_kernel, rope_cols=rope_cols),
        grid=(n_batch, tiles),
        in_specs=[pl.BlockSpec((None, ROW_TILE, D), lambda b, t: (b, t, 0)),
                  _mod_spec(layer, n_batch, _identity),
                  pl.BlockSpec((1, D), lambda b, t: (0, 0)),
                  pl.BlockSpec((D, n_out), lambda b, t: (0, 0)),
                  pl.BlockSpec((ROW_TILE, RET_DK), lambda b, t: (t, 0))],
        out_specs=pl.BlockSpec((None, ROW_TILE, n_out), lambda b, t: (b, t, 0)),
        out_shape=jax.ShapeDtypeStruct((n_batch, lc, n_out), BF16),
        compiler_params=_params("parallel", "parallel"),
    )(xc, mod4, g.reshape(1, D), w, cs)


def _rope_table(ctx_len, seq_len):
    n_freq = RET_DK // 4
    freqs = ROPE_BASE ** (-jnp.arange(n_freq, dtype=F32) / n_freq)
    t = jnp.arange(seq_len, dtype=jnp.int32)
    row, col = (t // GRID_W).astype(F32), (t % GRID_W).astype(F32)
    ang = jnp.concatenate([row[:, None] * freqs, col[:, None] * freqs], axis=-1)
    ang = jnp.concatenate([jnp.zeros((ctx_len, RET_DK // 2), F32), ang], axis=0)
    return jnp.concatenate([jnp.cos(ang), jnp.sin(ang)], axis=-1)


def _ret_table_kernel(dl_ref, tab_ref, cd_ref):
    c = RET_CHUNK
    i = lax.broadcasted_iota(jnp.int32, (c, c), 0).astype(F32)
    j = lax.broadcasted_iota(jnp.int32, (c, c), 1).astype(F32)

    def log_sigmoid(x):
        return jnp.minimum(x, 0.0) - jnp.log(1.0 + jnp.exp(-jnp.abs(x)))

    for h in range(RET_HEADS):
        lg_f = log_sigmoid(jnp.full((c, c), dl_ref[0, h], F32))
        lg_b = log_sigmoid(jnp.full((c, c), dl_ref[1, h], F32))
        fwd = jnp.where(i >= j, jnp.exp(jnp.maximum(i - j, 0.0) * lg_f), 0.0)
        bwd = jnp.where(j >= i, jnp.exp(jnp.maximum(j - i, 0.0) * lg_b), 0.0)
        tab_ref[h, 0] = fwd + bwd
        tab_ref[h, 1] = jnp.exp((i + 1.0) * lg_f)
        tab_ref[h, 2] = jnp.exp((c - 1.0 - i) * lg_f)
        tab_ref[h, 3] = jnp.exp((c - i) * lg_b)
        tab_ref[h, 4] = jnp.exp(i * lg_b)
        cd_f = jnp.exp(c * lg_f[:8])
        cd_b = jnp.exp(c * lg_b[:8])
        cd_ref[h, 0] = jnp.concatenate([cd_f] * (RET_DV // c), axis=1)
        cd_ref[h, 1] = jnp.concatenate([cd_b] * (RET_DV // c), axis=1)


def _ret_tables(decay_logit):
    c = RET_CHUNK
    return pl.pallas_call(
        _ret_table_kernel,
        in_specs=[pl.BlockSpec(memory_space=pltpu.SMEM)],
        out_shape=(jax.ShapeDtypeStruct((RET_HEADS, 5, c, c), F32),
                   jax.ShapeDtypeStruct((RET_HEADS, 2, 8, RET_DV), F32)),
    )(decay_logit)


def _lanes(t, n):
    return jnp.concatenate([t] * n, axis=1)


def _ret_fwd_kernel(q_ref, k_ref, v_ref, tab_ref, cd_ref, o_ref, s_ref):
    @pl.when(pl.program_id(1) == 0)
    def _():
        s_ref[...] = jnp.zeros_like(s_ref)

    c = RET_CHUNK
    for ci in range(ROW_TILE // c):
        rows = slice(ci * c, (ci + 1) * c)
        for h in range(RET_HEADS):
            qk = slice(h * RET_DK, (h + 1) * RET_DK)
            vv = slice(h * RET_DV, (h + 1) * RET_DV)
            q, k, v = q_ref[rows, qk], k_ref[rows, qk], v_ref[rows, vv]
            sc = lax.dot_general(q, k, _NT, preferred_element_type=F32) * tab_ref[h, 0]
            o = _bdot(sc, v)
            s = s_ref[h]
            o = o + _bdot(q, s) * _lanes(tab_ref[h, 1], RET_DV // c)
            kd = k.astype(F32) * _lanes(tab_ref[h, 2], RET_DK // c)
            s_ref[h] = s * cd_ref[h, 0, 0:1, :] + _bdot(kd, v, _TN)
            o_ref[rows, vv] = o


def _ret_bwd_kernel(q_ref, k_ref, v_ref, z_ref, op_ref, x_ref, mod_ref, wo_ref, tab_ref, cd_ref,
                    xo_ref, s_ref):
    @pl.when(pl.program_id(1) == 0)
    def _():
        s_ref[...] = jnp.zeros_like(s_ref)

    c = RET_CHUNK
    gate = mod_ref[:, 2 * D:]
    for ci in reversed(range(ROW_TILE // c)):
        rows = slice(ci * c, (ci + 1) * c)
        acc = jnp.zeros((c, D), F32)
        for h in range(RET_HEADS):
            qk = slice(h * RET_DK, (h + 1) * RET_DK)
            vv = slice(h * RET_DV, (h + 1) * RET_DV)
            q, k, v = q_ref[rows, qk], k_ref[rows, qk], v_ref[rows, vv]
            s = s_ref[h]
            o = op_ref[rows, vv] + _bdot(q, s) * _lanes(tab_ref[h, 3], RET_DV // c)
            kd = k.astype(F32) * _lanes(tab_ref[h, 4], RET_DK // c)
            s_ref[h] = s * cd_ref[h, 1, 0:1, :] + _bdot(kd, v, _TN)
            o = o * lax.rsqrt(jnp.mean(o * o, axis=-1, keepdims=True) + NORM_EPS)
            gz = o * _silu(z_ref[rows, vv].astype(F32))
            acc = acc + _bdot(gz, wo_ref[vv, :])
        xo_ref[rows, :] = x_ref[rows, :] + gate * acc


def _bwd_tile(tiles):
    return lambda s: jnp.where(s == 0, 0, tiles - s)


def _retention(xc, proj, mod4, w_out, tab, cd, layer):
    n_batch, lc, _ = xc.shape
    tiles = lc // ROW_TILE
    width = RET_HEADS * RET_DV
    state = pltpu.VMEM((RET_HEADS, RET_DK, RET_DV), F32)
    tab_spec = pl.BlockSpec(tab.shape, lambda b, s: (0, 0, 0, 0))
    cd_spec = pl.BlockSpec(cd.shape, lambda b, s: (0, 0, 0, 0))

    def col_spec(width_, col, tile_of_step):
        return pl.BlockSpec((None, ROW_TILE, width_), lambda b, s: (b, tile_of_step(s), col))

    o_part = pl.pallas_call(
        _ret_fwd_kernel,
        grid=(n_batch, tiles),
        in_specs=[col_spec(RET_HEADS * RET_DK, 0, _identity), col_spec(RET_HEADS * RET_DK, 1, _identity),
                  col_spec(width, 1, _identity), tab_spec, cd_spec],
        out_specs=col_spec(width, 0, _identity),
        out_shape=jax.ShapeDtypeStruct((n_batch, lc, width), F32),
        scratch_shapes=[state],
        compiler_params=_params("parallel", "arbitrary"),
    )(proj, proj, proj, tab, cd)

    bt = _bwd_tile(tiles)
    return pl.pallas_call(
        _ret_bwd_kernel,
        grid=(n_batch, tiles),
        in_specs=[col_spec(RET_HEADS * RET_DK, 0, bt), col_spec(RET_HEADS * RET_DK, 1, bt),
                  col_spec(width, 1, bt), col_spec(width, 2, bt), col_spec(width, 0, bt),
                  col_spec(D, 0, bt), _mod_spec(layer, n_batch, bt),
                  pl.BlockSpec((width, D), lambda b, s: (0, 0)), tab_spec, cd_spec],
        out_specs=col_spec(D, 0, bt),
        out_shape=jax.ShapeDtypeStruct(xc.shape, F32),
        scratch_shapes=[state],
        compiler_params=_params("parallel", "arbitrary"),
    )(proj, proj, proj, proj, o_part, xc, mod4, w_out, tab, cd)


def _gm_kernel(u_ref, v_ref, z_ref, x_ref, mod_ref, vg_ref, ws_ref, bs_ref, wo_ref, xo_ref):
    v = v_ref[...].astype(F32)
    v = v - jnp.mean(v, axis=-1, keepdims=True)
    vn = (v * lax.rsqrt(jnp.mean(v * v, axis=-1, keepdims=True) + NORM_EPS) * vg_ref[...]).astype(BF16)
    gate = mod_ref[:, 2 * D:]
    c = GM_CHUNK
    gw = GM_WIDTH // GM_GROUPS
    for ci in range(ROW_TILE // c):
        rows = slice(ci * c, (ci + 1) * c)
        acc = jnp.zeros((c, D), F32)
        for g in range(GM_GROUPS):
            cols = slice(g * gw, (g + 1) * gw)
            mixed = jnp.dot(ws_ref[g], vn[rows, cols], preferred_element_type=F32) + bs_ref[:, g:g + 1]
            y = u_ref[rows, cols].astype(F32) * mixed * _silu(z_ref[rows, cols].astype(F32))
            acc = acc + _bdot(y, wo_ref[cols, :])
        xo_ref[rows, :] = x_ref[rows, :] + gate * acc


def _gmlp(xc, proj, mod4, vnorm_g, w_s, b_s, w_out, layer):
    n_batch, lc, _ = xc.shape
    tiles = lc // ROW_TILE

    def col_spec(width, col):
        return pl.BlockSpec((None, ROW_TILE, width), lambda b, t: (b, t, col))

    def full(a):
        return pl.BlockSpec(a.shape, lambda b, t: (0,) * a.ndim)

    return pl.pallas_call(
        _gm_kernel,
        grid=(n_batch, tiles),
        in_specs=[col_spec(GM_WIDTH, 0), col_spec(GM_WIDTH, 1), col_spec(GM_WIDTH, 2), col_spec(D, 0),
                  _mod_spec(layer, n_batch, _identity), full(vnorm_g), full(w_s), full(b_s), full(w_out)],
        out_specs=col_spec(D, 0),
        out_shape=jax.ShapeDtypeStruct(xc.shape, F32),
        compiler_params=_params("parallel", "parallel"),
    )(proj, proj, proj, xc, mod4, vnorm_g, w_s, b_s, w_out)


def _head_sum(x, e_ref):
    outs = []
    e = e_ref[...]
    for g in range(x.shape[-1] // WKV_LANES):
        hi, mid, lo = _split3(x[:, g * WKV_LANES:(g + 1) * WKV_LANES])
        outs.append(jnp.dot(hi, e, preferred_element_type=F32) + jnp.dot(mid, e, preferred_element_type=F32)
                    + jnp.dot(lo, e, preferred_element_type=F32))
    return jnp.concatenate(outs, axis=1)


def _rw_prep_kernel(xp_ref, x_ref, xn_ref, mod_ref, g_ref, mu_ref, wm_ref, w1_ref, w2_ref, a1_ref, a2_ref,
                    vec_ref, e_ref, r_ref, v_ref, kk_ref, k_ref, b_ref, ld_ref, bonus_ref, z_ref,
                    *, grid_rows):
    t = pl.program_id(1)
    is_ctx = t == 0
    g, mod = g_ref[...], mod_ref[...]
    h = _norm_mod(x_ref[...], g, mod)
    h_up = _norm_mod(xp_ref[...], g, mod)
    h_dn = _norm_mod(xn_ref[...], g, mod)
    n = ROW_TILE
    prev = pltpu.roll(h, 1, 0)
    nxt = pltpu.roll(h, n - 1, 0)
    ext = jnp.concatenate([h_up, h, h_dn], axis=0)
    up, down = ext[:n], ext[2 * GRID_W:]

    lane_q = lax.broadcasted_iota(jnp.int32, (n, D), 1) // (D // 4)
    i = lax.broadcasted_iota(jnp.int32, (n, D), 0)
    grow = (t - 1) * (n // GRID_W) + i // GRID_W
    pos = jnp.where(is_ctx, i, i % GRID_W)
    last_pos = jnp.where(is_ctx, n - 1, GRID_W - 1)
    prev_lanes = jnp.where(is_ctx, 2, 1)
    next_lanes = jnp.where(is_ctx, 4, 2)
    prev = jnp.where(pos > 0, prev, 0.0)
    nxt = jnp.where(pos < last_pos, nxt, 0.0)
    up = jnp.where(grow > 0, up, 0.0)
    down = jnp.where(grow < grid_rows - 1, down, 0.0)
    shifted = jnp.where(lane_q < prev_lanes, prev,
                        jnp.where(lane_q < next_lanes, nxt, jnp.where(lane_q == 2, up, down)))
    xx = shifted - h

    def mix(p):
        return (h + xx * mu_ref[p:p + 1, :]).astype(BF16)

    vec = vec_ref[...]
    r = jnp.dot(mix(0), wm_ref[0], preferred_element_type=F32)
    k = jnp.dot(mix(2), wm_ref[1], preferred_element_type=F32)
    v = jnp.dot(mix(3), wm_ref[2], preferred_element_type=F32)
    z_ref[...] = jnp.dot(mix(5), wm_ref[3], preferred_element_type=F32).astype(BF16)
    tw = jnp.tanh(jnp.dot(mix(1), w1_ref[...], preferred_element_type=F32))
    ta = jnp.dot(mix(4), a1_ref[...], preferred_element_type=F32)

    kk = k * vec[4:5]
    kk = kk / jnp.maximum(jnp.sqrt(_head_sum(kk * kk, e_ref)), 1e-12)
    r_ref[...] = r
    v_ref[...] = v
    kk_ref[...] = kk
    bonus = jnp.zeros_like(r)
    for d in range(2):
        lw = vec[d:d + 1] + _bdot(tw, w2_ref[d])
        w_log = -(jnp.maximum(-lw, 0.0) + jnp.log(1.0 + jnp.exp(-jnp.abs(lw)))) - 0.5
        ld_ref[d] = -jnp.exp(w_log)
        a = 1.0 / (1.0 + jnp.exp(-(vec[2 + d:3 + d] + _bdot(ta, a2_ref[d]))))
        kd = k * (1.0 + (a - 1.0) * vec[5:6])
        k_ref[d] = kd
        b_ref[d] = kk * a
        bonus = bonus + r * kd * vec[6:7]
    bonus_ref[...] = _head_sum(bonus, e_ref) * v


def _rw_prepare(xc, mod4, g, mu, wm, w1, w2, a1, a2, vec, e, layer, grid_rows):
    n_batch, lc, _ = xc.shape
    tiles = lc // ROW_TILE
    per_tile = ROW_TILE // GRID_W
    last = lc // GRID_W - 1

    def full(a):
        return pl.BlockSpec(a.shape, lambda b, t: (0,) * a.ndim)

    tile_spec = pl.BlockSpec((None, ROW_TILE, D), lambda b, t: (b, t, 0))
    dir_spec = pl.BlockSpec((2, None, ROW_TILE, D), lambda b, t: (0, b, t, 0))
    one = jax.ShapeDtypeStruct((n_batch, lc, D), F32)
    two = jax.ShapeDtypeStruct((2, n_batch, lc, D), F32)
    return pl.pallas_call(
        functools.partial(_rw_prep_kernel, grid_rows=grid_rows),
        grid=(n_batch, tiles),
        in_specs=[pl.BlockSpec((None, GRID_W, D), lambda b, t: (b, jnp.maximum(t * per_tile - 1, 0), 0)),
                  tile_spec,
                  pl.BlockSpec((None, GRID_W, D), lambda b, t: (b, jnp.minimum((t + 1) * per_tile, last), 0)),
                  _mod_spec(layer, n_batch, _identity), full(g), full(mu), full(wm), full(w1), full(w2),
                  full(a1), full(a2), full(vec), full(e)],
        out_specs=[tile_spec, tile_spec, tile_spec, dir_spec, dir_spec, dir_spec, tile_spec, tile_spec],
        out_shape=[one, one, one, two, two, two, one, jax.ShapeDtypeStruct((n_batch, lc, D), BF16)],
        compiler_params=_params("parallel", "parallel"),
    )(xc, xc, xc, mod4, g, mu, wm, w1, w2, a1, a2, vec, e)


def _wkv_kernel(r_ref, kk_ref, b_ref, k_ref, v_ref, ld_ref, yin_ref, y_ref, ht_ref, *, reverse):
    @pl.when(pl.program_id(2) == 0)
    def _():
        ht_ref[...] = jnp.zeros_like(ht_ref)

    c, w, hg = WKV_CHUNK, WKV_LANES, WKV_GROUP
    n_chunks = ROW_TILE // c
    rows_i = lax.broadcasted_iota(jnp.int32, (hg * c, hg * c), 0)
    cols_i = lax.broadcasted_iota(jnp.int32, (hg * c, hg * c), 1)
    same_head = (rows_i // c) == (cols_i // c)
    ti, si = rows_i % c, cols_i % c
    before = (si > ti) if reverse else (si < ti)
    strict_bd = jnp.where(same_head & before, 1.0, 0.0)
    eye_bd = jnp.where(rows_i == cols_i, 1.0, 0.0)
    t1 = lax.broadcasted_iota(jnp.int32, (hg * c, c), 0) % c
    s1 = lax.broadcasted_iota(jnp.int32, (hg * c, c), 1)
    strict = jnp.where((s1 > t1) if reverse else (s1 < t1), 1.0, 0.0)
    incl = jnp.where((s1 >= t1) if reverse else (s1 <= t1), 1.0, 0.0)
    incl2 = jnp.concatenate([incl, incl], axis=1)
    cum_mat = incl[:c].astype(BF16)
    lane_head = lax.broadcasted_iota(jnp.int32, (hg * c, w), 1) // RW_HEAD
    row_head = lax.broadcasted_iota(jnp.int32, (hg * c, w), 0) // c
    m4 = jnp.where(lane_head == row_head, 1.0, 0.0)
    bd_state = jnp.where(lax.broadcasted_iota(jnp.int32, (w, w), 0) // RW_HEAD
                         == lax.broadcasted_iota(jnp.int32, (w, w), 1) // RW_HEAD, 1.0, 0.0)

    def stack(x):
        return jnp.concatenate([x] * hg, axis=0) * m4

    def fold(x):
        return x[:c] + x[c:2 * c] + x[2 * c:3 * c] + x[3 * c:]

    def chunk(step, carry):
        ci = (n_chunks - 1 - step) if reverse else step
        rows = pl.ds(pl.multiple_of(ci * c, c), c)
        ld = ld_ref[rows, :]
        hi, mid, lo = _split3(ld)
        cum = (jnp.dot(cum_mat, hi, preferred_element_type=F32) + jnp.dot(cum_mat, mid, preferred_element_type=F32)
               + jnp.dot(cum_mat, lo, preferred_element_type=F32))
        tot = cum[0:1] if reverse else cum[c - 1:c]
        e_in = jnp.exp(cum)
        e_out = jnp.exp(-cum)
        e_end = jnp.exp(tot - cum)
        r, kk, b, k, v = r_ref[rows, :], kk_ref[rows, :], b_ref[rows, :], k_ref[rows, :], v_ref[rows, :]
        rt = r * e_in
        at = -kk * jnp.exp(cum - ld)
        kt = (k * e_out).astype(BF16)
        bt = (b * e_out).astype(BF16)
        la = stack(at).astype(BF16)
        lr = stack(rt).astype(BF16)
        lb = stack(bt.astype(F32)).astype(BF16)
        vb = v.astype(BF16)

        n_bd = lax.dot_general(la, lb, _NT, preferred_element_type=F32) * strict_bd
        a_ak = lax.dot_general(la, kt, _NT, preferred_element_type=F32) * strict
        a_r = lax.dot_general(lr, jnp.concatenate([kt, bt], axis=0), _NT,
                              preferred_element_type=F32) * incl2
        tinv = eye_bd + n_bd
        p = n_bd
        for _ in range(c.bit_length() - 2):
            p = _bdot(p, p)
            tinv = tinv + _bdot(p, tinv)
        akv = _bdot(a_ak, vb) * m4
        tx = _bdot(tinv, jnp.concatenate([la, akv.astype(BF16)], axis=1))
        wu = fold(tx[:, :w])
        u0 = fold(tx[:, w:])

        ht = ht_ref[...]
        wrh = lax.dot_general(jnp.concatenate([wu.astype(BF16), rt.astype(BF16)], axis=0), ht.astype(BF16),
                              _NT, preferred_element_type=F32)
        u = wrh[:c] + u0
        vu = jnp.concatenate([vb, u.astype(BF16)], axis=0)
        y = wrh[c:] + fold(_bdot(a_r, vu) * m4)
        kb = jnp.concatenate([(k * e_end).astype(BF16), (b * e_end).astype(BF16)], axis=0)
        ht_ref[...] = (ht * jnp.exp(tot) + lax.dot_general(vu, kb, _TN, preferred_element_type=F32)) * bd_state
        y_ref[rows, :] = y + yin_ref[rows, :]
        return carry

    lax.fori_loop(0, n_chunks, chunk, 0)


def _wkv(r, kk, b, k, v, ld, y_in, direction):
    n_batch, lc, _ = r.shape
    tiles = lc // ROW_TILE
    groups = D // WKV_LANES
    reverse = direction == 1
    tile_of = _bwd_tile(tiles) if reverse else _identity
    spec = pl.BlockSpec((None, ROW_TILE, WKV_LANES), lambda bb, g, s: (bb, tile_of(s), g))
    dspec = pl.BlockSpec((None, None, ROW_TILE, WKV_LANES), lambda bb, g, s: (direction, bb, tile_of(s), g))
    return pl.pallas_call(
        functools.partial(_wkv_kernel, reverse=reverse),
        grid=(n_batch, groups, tiles),
        in_specs=[spec, spec, dspec, dspec, spec, dspec, spec],
        out_specs=spec,
        out_shape=jax.ShapeDtypeStruct(r.shape, F32),
        scratch_shapes=[pltpu.VMEM((WKV_LANES, WKV_LANES), F32)],
        compiler_params=_params("parallel", "parallel", "arbitrary"),
    )(r, kk, b, k, v, ld, y_in)


def _rw_out_kernel(y_ref, bonus_ref, z_ref, x_ref, mod_ref, vec_ref, e_ref, wo_ref, xo_ref):
    y = y_ref[...]
    inv = 1.0 / RW_HEAD
    yc = y - _head_sum(y, e_ref) * inv
    yn = yc * lax.rsqrt(_head_sum(yc * yc, e_ref) * inv + RW_LNX_EPS)
    yn = yn * vec_ref[0:1] + vec_ref[1:2]
    o = (yn + bonus_ref[...]) * _silu(z_ref[...].astype(F32))
    xo_ref[...] = x_ref[...] + mod_ref[:, 2 * D:] * _bdot(o, wo_ref[...])


def _rw_output(xc, y, bonus, z, mod4, vec, e, w_out, layer):
    n_batch, lc, _ = xc.shape
    tiles = lc // ROW_TILE
    tile_spec = pl.BlockSpec((None, ROW_TILE, D), lambda b, t: (b, t, 0))

    def full(a):
        return pl.BlockSpec(a.shape, lambda b, t: (0,) * a.ndim)

    return pl.pallas_call(
        _rw_out_kernel,
        grid=(n_batch, tiles),
        in_specs=[tile_spec, tile_spec, tile_spec, tile_spec, _mod_spec(layer, n_batch, _identity),
                  full(vec), full(e), full(w_out)],
        out_specs=tile_spec,
        out_shape=jax.ShapeDtypeStruct(xc.shape, F32),
        compiler_params=_params("parallel", "parallel"),
    )(y, bonus, z, xc, mod4, vec, e, w_out)


def _rwkv(xc, mod4, g, mu, w_rkvg, w0, w1, w2, a0, a1, a2, k_k, k_a, r_k, lnx_g, lnx_b, w_out, layer, grid_rows):
    zeros = jnp.zeros((RW_LORA, D), F32)
    w1c = jnp.concatenate([w1[0], w1[1]], axis=1).astype(BF16)
    a1c = jnp.concatenate([a1[0], a1[1]], axis=1).astype(BF16)
    w2p = jnp.stack([jnp.concatenate([w2[0], zeros]), jnp.concatenate([zeros, w2[1]])]).astype(BF16)
    a2p = jnp.stack([jnp.concatenate([a2[0], zeros]), jnp.concatenate([zeros, a2[1]])]).astype(BF16)
    zrow = jnp.zeros((D,), F32)
    vec = jnp.stack([w0[0], w0[1], a0[0], a0[1], k_k, k_a, r_k.reshape(D), zrow])
    lane = jnp.arange(WKV_LANES) // RW_HEAD
    e = (lane[:, None] == lane[None, :]).astype(BF16)
    r, v, kk, k, b, ld, bonus, z = _rw_prepare(xc, mod4, g.reshape(1, D), mu, w_rkvg.astype(BF16), w1c, w2p,
                                               a1c, a2p, vec, e, layer, grid_rows)
    y = _wkv(r, kk, b, k, v, ld, jnp.zeros_like(r), 0)
    y = _wkv(r, kk, b, k, v, ld, y, 1)
    vec_o = jnp.stack([lnx_g, lnx_b] + [zrow] * 6)
    return _rw_output(xc, y, bonus, z, mod4, vec_o, e, w_out.astype(BF16), layer)


def _final_kernel(x_ref, g_ref, o_ref):
    x = x_ref[...]
    o_ref[...] = x * lax.rsqrt(jnp.mean(x * x, axis=-1, keepdims=True) + NORM_EPS) * g_ref[...]


def _final_norm(xc, g, ctx_tiles):
    n_batch, lc, _ = xc.shape
    tiles = lc // ROW_TILE - ctx_tiles
    return pl.pallas_call(
        _final_kernel,
        grid=(n_batch, tiles),
        in_specs=[pl.BlockSpec((None, ROW_TILE, D), lambda b, t: (b, t + ctx_tiles, 0)),
                  pl.BlockSpec((1, D), lambda b, t: (0, 0))],
        out_specs=pl.BlockSpec((None, ROW_TILE, D), lambda b, t: (b, t, 0)),
        out_shape=jax.ShapeDtypeStruct((n_batch, tiles * ROW_TILE, D), F32),
        compiler_params=_params("parallel", "parallel"),
    )(xc, g.reshape(1, D))


def kernel(x, c, ctx, c_ctx, ada_w, ada_b, norm_g, final_g, ret_w_in, ret_decay, ret_w_out, gm_w_in, gm_vnorm_g,
           gm_w_s, gm_b_s, gm_w_out, rw_mu, rw_w_rkvg, rw_w0, rw_w1, rw_w2, rw_a0, rw_a1, rw_a2, rw_k_k, rw_k_a,
           rw_r_k, rw_lnx_g, rw_lnx_b, rw_w_out):
    n_batch, seq_len, _ = x.shape
    ctx_len = ctx.shape[1]
    depth = ada_w.shape[0]
    assert ctx_len == ROW_TILE and seq_len % ROW_TILE == 0 and n_batch < 8

    xc = jnp.concatenate([ctx, x], axis=1)
    cvec = jnp.zeros((8, D), F32).at[:n_batch].set(c).at[n_batch].set(c_ctx)
    mod4 = _modulation(cvec, ada_w, ada_b).reshape(depth, 8, 1, 3 * D)
    rope = _rope_table(ctx_len, seq_len)
    k_scale = jnp.concatenate([jnp.ones((RET_HEADS * RET_DK,), F32),
                               jnp.full((RET_HEADS * RET_DK,), RET_DK ** -0.5, F32),
                               jnp.ones((2 * RET_HEADS * RET_DV,), F32)])

    for i in range(depth):
        kind, j = i % N_MIXERS, i // N_MIXERS
        if kind == 0:
            w_in = (ret_w_in[j] * k_scale).astype(BF16)
            proj = _project(xc, mod4, norm_g[i], w_in, rope, i, 2 * RET_HEADS * RET_DK)
            tab, cd = _ret_tables(ret_decay[j])
            xc = _retention(xc, proj, mod4, ret_w_out[j].astype(BF16), tab, cd, i)
        elif kind == 1:
            proj = _project(xc, mod4, norm_g[i], gm_w_in[j].astype(BF16), rope, i, 0)
            xc = _gmlp(xc, proj, mod4, gm_vnorm_g[j].reshape(1, GM_WIDTH), gm_w_s[j].astype(BF16),
                       gm_b_s[j].T, gm_w_out[j].astype(BF16), i)
        else:
            xc = _rwkv(xc, mod4, norm_g[i], rw_mu[j], rw_w_rkvg[j], rw_w0[j], rw_w1[j], rw_w2[j], rw_a0[j],
                       rw_a1[j], rw_a2[j], rw_k_k[j], rw_k_a[j], rw_r_k[j], rw_lnx_g[j], rw_lnx_b[j],
                       rw_w_out[j], i, seq_len // GRID_W)
    return _final_norm(xc, final_g, ctx_len // ROW_TILE)
```

```python
import functools

import jax
import jax.numpy as jnp
from jax import lax
from jax.experimental import pallas as pl
from jax.experimental.pallas import tpu as pltpu

F32 = jnp.float32
BF16 = jnp.bfloat16

D = 1024
GRID_W = 64
NORM_EPS = 1e-6
N_MIXERS = 3
ROW_TILE = 256
RET_HEADS = 4
RET_DK = 256
RET_DV = 512
RET_CHUNK = 128
ROPE_BASE = 10000.0
GM_WIDTH = 2 * D
GM_GROUPS = 8
GM_CHUNK = 128
RW_HEAD = 64
RW_LORA = 64
RW_LNX_EPS = 64e-5
WKV_CHUNK = 64
WKV_LANES = 256
WKV_GROUP = WKV_LANES // RW_HEAD
VMEM_LIMIT = 56 * 1024 * 1024

_NT = (((1,), (1,)), ((), ()))
_TN = (((0,), (0,)), ((), ()))


def _params(*sem):
    return pltpu.CompilerParams(dimension_semantics=sem, vmem_limit_bytes=VMEM_LIMIT)


def _silu(x):
    return x / (1.0 + jnp.exp(-x))


def _bdot(a, b, dims=None):
    a = a.astype(BF16)
    b = b.astype(BF16)
    if dims is None:
        return jnp.dot(a, b, preferred_element_type=F32)
    return lax.dot_general(a, b, dims, preferred_element_type=F32)


def _split3(x):
    hi = x.astype(BF16)
    r1 = x - hi.astype(F32)
    mid = r1.astype(BF16)
    lo = (r1 - mid.astype(F32)).astype(BF16)
    return hi, mid, lo


def _norm_mod(x, g, mod):
    ms = jnp.mean(x * x, axis=-1, keepdims=True)
    y = x * lax.rsqrt(ms + NORM_EPS) * g
    return y * (1.0 + mod[:, D:2 * D]) + mod[:, :D]


def _mod_spec(layer, n_batch, tile_of_step):
    def index(b, s):
        t = tile_of_step(s)
        return (layer, jnp.where(t == 0, n_batch, b), 0, 0)
    return pl.BlockSpec((None, None, 1, 3 * D), index)


def _identity(s):
    return s


def _mod_kernel(c_ref, w_ref, b_ref, o_ref):
    s = _silu(c_ref[...])
    o_ref[...] = jnp.dot(s, w_ref[...], preferred_element_type=F32,
                         precision=lax.Precision.HIGHEST) + b_ref[...]


def _modulation(cvec, ada_w, ada_b):
    depth = ada_w.shape[0]
    return pl.pallas_call(
        _mod_kernel,
        grid=(depth, 3),
        in_specs=[pl.BlockSpec((8, D), lambda i, j: (0, 0)),
                  pl.BlockSpec((None, D, D), lambda i, j: (i, 0, j)),
                  pl.BlockSpec((None, 1, D), lambda i, j: (i, 0, j))],
        out_specs=pl.BlockSpec((None, 8, D), lambda i, j: (i, 0, j)),
        out_shape=jax.ShapeDtypeStruct((depth, 8, 3 * D), F32),
        compiler_params=_params("arbitrary", "arbitrary"),
    )(cvec, ada_w, ada_b.reshape(depth, 1, 3 * D))


def _proj_kernel(x_ref, mod_ref, g_ref, w_ref, cs_ref, o_ref, *, rope_cols):
    h = _norm_mod(x_ref[...], g_ref[...], mod_ref[...]).astype(BF16)
    n_out = o_ref.shape[-1]
    half = RET_DK // 2
    cos = cs_ref[:, :half]
    sin = cs_ref[:, half:]
    for c0 in range(0, rope_cols, RET_DK):
        acc = jnp.dot(h, w_ref[:, c0:c0 + RET_DK], preferred_element_type=F32)
        t1, t2 = acc[:, :half], acc[:, half:]
        o_ref[:, c0:c0 + half] = (t1 * cos - t2 * sin).astype(BF16)
        o_ref[:, c0 + half:c0 + RET_DK] = (t1 * sin + t2 * cos).astype(BF16)
    nb = 512
    for c0 in range(rope_cols, n_out, nb):
        o_ref[:, c0:c0 + nb] = jnp.dot(h, w_ref[:, c0:c0 + nb], preferred_element_type=F32).astype(BF16)


def _project(xc, mod4, g, w, cs, layer, rope_cols):
    n_batch, lc, _ = xc.shape
    n_out = w.shape[1]
    tiles = lc // ROW_TILE
    return pl.pallas_call(
        functools.partial(_proj_kernel, rope_cols=rope_cols),
        grid=(n_batch, tiles),
        in_specs=[pl.BlockSpec((None, ROW_TILE, D), lambda b, t: (b, t, 0)),
                  _mod_spec(layer, n_batch, _identity),
                  pl.BlockSpec((1, D), lambda b, t: (0, 0)),
                  pl.BlockSpec((D, n_out), lambda b, t: (0, 0)),
                  pl.BlockSpec((ROW_TILE, RET_DK), lambda b, t: (t, 0))],
        out_specs=pl.BlockSpec((None, ROW_TILE, n_out), lambda b, t: (b, t, 0)),
        out_shape=jax.ShapeDtypeStruct((n_batch, lc, n_out), BF16),
        compiler_params=_params("parallel", "parallel"),
    )(xc, mod4, g.reshape(1, D), w, cs)


def _rope_table(ctx_len, seq_len):
    n_freq = RET_DK // 4
    freqs = ROPE_BASE ** (-jnp.arange(n_freq, dtype=F32) / n_freq)
    t = jnp.arange(seq_len, dtype=jnp.int32)
    row, col = (t // GRID_W).astype(F32), (t % GRID_W).astype(F32)
    ang = jnp.concatenate([row[:, None] * freqs, col[:, None] * freqs], axis=-1)
    ang = jnp.concatenate([jnp.zeros((ctx_len, RET_DK // 2), F32), ang], axis=0)
    return jnp.concatenate([jnp.cos(ang), jnp.sin(ang)], axis=-1)


def _ret_table_kernel(dl_ref, tab_ref, cd_ref):
    c = RET_CHUNK
    i = lax.broadcasted_iota(jnp.int32, (c, c), 0).astype(F32)
    j = lax.broadcasted_iota(jnp.int32, (c, c), 1).astype(F32)

    def log_sigmoid(x):
        return jnp.minimum(x, 0.0) - jnp.log(1.0 + jnp.exp(-jnp.abs(x)))

    for h in range(RET_HEADS):
        lg_f = log_sigmoid(jnp.full((c, c), dl_ref[0, h], F32))
        lg_b = log_sigmoid(jnp.full((c, c), dl_ref[1, h], F32))
        fwd = jnp.where(i >= j, jnp.exp(jnp.maximum(i - j, 0.0) * lg_f), 0.0)
        bwd = jnp.where(j >= i, jnp.exp(jnp.maximum(j - i, 0.0) * lg_b), 0.0)
        tab_ref[h, 0] = fwd + bwd
        tab_ref[h, 1] = jnp.exp((i + 1.0) * lg_f)
        tab_ref[h, 2] = jnp.exp((c - 1.0 - i) * lg_f)
        tab_ref[h, 3] = jnp.exp((c - i) * lg_b)
        tab_ref[h, 4] = jnp.exp(i * lg_b)
        cd_f = jnp.exp(c * lg_f[:8])
        cd_b = jnp.exp(c * lg_b[:8])
        cd_ref[h, 0] = jnp.concatenate([cd_f] * (RET_DV // c), axis=1)
        cd_ref[h, 1] = jnp.concatenate([cd_b] * (RET_DV // c), axis=1)


def _ret_tables(decay_logit):
    c = RET_CHUNK
    return pl.pallas_call(
        _ret_table_kernel,
        in_specs=[pl.BlockSpec(memory_space=pltpu.SMEM)],
        out_shape=(jax.ShapeDtypeStruct((RET_HEADS, 5, c, c), F32),
                   jax.ShapeDtypeStruct((RET_HEADS, 2, 8, RET_DV), F32)),
    )(decay_logit)


def _lanes(t, n):
    return jnp.concatenate([t] * n, axis=1)


def _ret_fwd_kernel(q_ref, k_ref, v_ref, tab_ref, cd_ref, o_ref, s_ref):
    @pl.when(pl.program_id(1) == 0)
    def _():
        s_ref[...] = jnp.zeros_like(s_ref)

    c = RET_CHUNK
    for ci in range(ROW_TILE // c):
        rows = slice(ci * c, (ci + 1) * c)
        for h in range(RET_HEADS):
            qk = slice(h * RET_DK, (h + 1) * RET_DK)
            vv = slice(h * RET_DV, (h + 1) * RET_DV)
            q, k, v = q_ref[rows, qk], k_ref[rows, qk], v_ref[rows, vv]
            sc = lax.dot_general(q, k, _NT, preferred_element_type=F32) * tab_ref[h, 0]
            o = _bdot(sc, v)
            s = s_ref[h]
            o = o + _bdot(q, s) * _lanes(tab_ref[h, 1], RET_DV // c)
            kd = k.astype(F32) * _lanes(tab_ref[h, 2], RET_DK // c)
            s_ref[h] = s * cd_ref[h, 0, 0:1, :] + _bdot(kd, v, _TN)
            o_ref[rows, vv] = o


def _ret_bwd_kernel(q_ref, k_ref, v_ref, z_ref, op_ref, x_ref, mod_ref, wo_ref, tab_ref, cd_ref,
                    xo_ref, s_ref):
    @pl.when(pl.program_id(1) == 0)
    def _():
        s_ref[...] = jnp.zeros_like(s_ref)

    c = RET_CHUNK
    gate = mod_ref[:, 2 * D:]
    for ci in reversed(range(ROW_TILE // c)):
        rows = slice(ci * c, (ci + 1) * c)
        acc = jnp.zeros((c, D), F32)
        for h in range(RET_HEADS):
            qk = slice(h * RET_DK, (h + 1) * RET_DK)
            vv = slice(h * RET_DV, (h + 1) * RET_DV)
            q, k, v = q_ref[rows, qk], k_ref[rows, qk], v_ref[rows, vv]
            s = s_ref[h]
            o = op_ref[rows, vv] + _bdot(q, s) * _lanes(tab_ref[h, 3], RET_DV // c)
            kd = k.astype(F32) * _lanes(tab_ref[h, 4], RET_DK // c)
            s_ref[h] = s * cd_ref[h, 1, 0:1, :] + _bdot(kd, v, _TN)
            o = o * lax.rsqrt(jnp.mean(o * o, axis=-1, keepdims=True) + NORM_EPS)
            gz = o * _silu(z_ref[rows, vv].astype(F32))
            acc = acc + _bdot(gz, wo_ref[vv, :])
        xo_ref[rows, :] = x_ref[rows, :] + gate * acc


def _bwd_tile(tiles):
    return lambda s: jnp.where(s == 0, 0, tiles - s)


def _retention(xc, proj, mod4, w_out, tab, cd, layer):
    n_batch, lc, _ = xc.shape
    tiles = lc // ROW_TILE
    width = RET_HEADS * RET_DV
    state = pltpu.VMEM((RET_HEADS, RET_DK, RET_DV), F32)
    tab_spec = pl.BlockSpec(tab.shape, lambda b, s: (0, 0, 0, 0))
    cd_spec = pl.BlockSpec(cd.shape, lambda b, s: (0, 0, 0, 0))

    def col_spec(width_, col, tile_of_step):
        return pl.BlockSpec((None, ROW_TILE, width_), lambda b, s: (b, tile_of_step(s), col))

    o_part = pl.pallas_call(
        _ret_fwd_kernel,
        grid=(n_batch, tiles),
        in_specs=[col_spec(RET_HEADS * RET_DK, 0, _identity), col_spec(RET_HEADS * RET_DK, 1, _identity),
                  col_spec(width, 1, _identity), tab_spec, cd_spec],
        out_specs=col_spec(width, 0, _identity),
        out_shape=jax.ShapeDtypeStruct((n_batch, lc, width), F32),
        scratch_shapes=[state],
        compiler_params=_params("parallel", "arbitrary"),
    )(proj, proj, proj, tab, cd)

    bt = _bwd_tile(tiles)
    return pl.pallas_call(
        _ret_bwd_kernel,
        grid=(n_batch, tiles),
        in_specs=[col_spec(RET_HEADS * RET_DK, 0, bt), col_spec(RET_HEADS * RET_DK, 1, bt),
                  col_spec(width, 1, bt), col_spec(width, 2, bt), col_spec(width, 0, bt),
                  col_spec(D, 0, bt), _mod_spec(layer, n_batch, bt),
                  pl.BlockSpec((width, D), lambda b, s: (0, 0)), tab_spec, cd_spec],
        out_specs=col_spec(D, 0, bt),
        out_shape=jax.ShapeDtypeStruct(xc.shape, F32),
        scratch_shapes=[state],
        compiler_params=_params("parallel", "arbitrary"),
    )(proj, proj, proj, proj, o_part, xc, mod4, w_out, tab, cd)


def _gm_kernel(u_ref, v_ref, z_ref, x_ref, mod_ref, vg_ref, ws_ref, bs_ref, wo_ref, xo_ref):
    v = v_ref[...].astype(F32)
    v = v - jnp.mean(v, axis=-1, keepdims=True)
    vn = (v * lax.rsqrt(jnp.mean(v * v, axis=-1, keepdims=True) + NORM_EPS) * vg_ref[...]).astype(BF16)
    gate = mod_ref[:, 2 * D:]
    c = GM_CHUNK
    gw = GM_WIDTH // GM_GROUPS
    for ci in range(ROW_TILE // c):
        rows = slice(ci * c, (ci + 1) * c)
        acc = jnp.zeros((c, D), F32)
        for g in range(GM_GROUPS):
            cols = slice(g * gw, (g + 1) * gw)
            mixed = jnp.dot(ws_ref[g], vn[rows, cols], preferred_element_type=F32) + bs_ref[:, g:g + 1]
            y = u_ref[rows, cols].astype(F32) * mixed * _silu(z_ref[rows, cols].astype(F32))
            acc = acc + _bdot(y, wo_ref[cols, :])
        xo_ref[rows, :] = x_ref[rows, :] + gate * acc


def _gmlp(xc, proj, mod4, vnorm_g, w_s, b_s, w_out, layer):
    n_batch, lc, _ = xc.shape
    tiles = lc // ROW_TILE

    def col_spec(width, col):
        return pl.BlockSpec((None, ROW_TILE, width), lambda b, t: (b, t, col))

    def full(a):
        return pl.BlockSpec(a.shape, lambda b, t: (0,) * a.ndim)

    return pl.pallas_call(
        _gm_kernel,
        grid=(n_batch, tiles),
        in_specs=[col_spec(GM_WIDTH, 0), col_spec(GM_WIDTH, 1), col_spec(GM_WIDTH, 2), col_spec(D, 0),
                  _mod_spec(layer, n_batch, _identity), full(vnorm_g), full(w_s), full(b_s), full(w_out)],
        out_specs=col_spec(D, 0),
        out_shape=jax.ShapeDtypeStruct(xc.shape, F32),
        compiler_params=_params("parallel", "parallel"),
    )(proj, proj, proj, xc, mod4, vnorm_g, w_s, b_s, w_out)


def _head_sum(x, e_ref):
    outs = []
    e = e_ref[...]
    for g in range(x.shape[-1] // WKV_LANES):
        hi, mid, lo = _split3(x[:, g * WKV_LANES:(g + 1) * WKV_LANES])
        outs.append(jnp.dot(hi, e, preferred_element_type=F32) + jnp.dot(mid, e, preferred_element_type=F32)
                    + jnp.dot(lo, e, preferred_element_type=F32))
    return jnp.concatenate(outs, axis=1)


def _rw_prep_kernel(xp_ref, x_ref, xn_ref, mod_ref, g_ref, mu_ref, wm_ref, w1_ref, w2_ref, a1_ref, a2_ref,
                    vec_ref, e_ref, r_ref, v_ref, kk_ref, k_ref, b_ref, ld_ref, bonus_ref, z_ref,
                    *, grid_rows):
    t = pl.program_id(1)
    is_ctx = t == 0
    g, mod = g_ref[...], mod_ref[...]
    h = _norm_mod(x_ref[...], g, mod)
    h_up = _norm_mod(xp_ref[...], g, mod)
    h_dn = _norm_mod(xn_ref[...], g, mod)
    n = ROW_TILE
    prev = pltpu.roll(h, 1, 0)
    nxt = pltpu.roll(h, n - 1, 0)
    ext = jnp.concatenate([h_up, h, h_dn], axis=0)
    up, down = ext[:n], ext[2 * GRID_W:]

    lane_q = lax.broadcasted_iota(jnp.int32, (n, D), 1) // (D // 4)
    i = lax.broadcasted_iota(jnp.int32, (n, D), 0)
    grow = (t - 1) * (n // GRID_W) + i // GRID_W
    pos = jnp.where(is_ctx, i, i % GRID_W)
    last_pos = jnp.where(is_ctx, n - 1, GRID_W - 1)
    prev_lanes = jnp.where(is_ctx, 2, 1)
    next_lanes = jnp.where(is_ctx, 4, 2)
    prev = jnp.where(pos > 0, prev, 0.0)
    nxt = jnp.where(pos < last_pos, nxt, 0.0)
    up = jnp.where(grow > 0, up, 0.0)
    down = jnp.where(grow < grid_rows - 1, down, 0.0)
    shifted = jnp.where(lane_q < prev_lanes, prev,
                        jnp.where(lane_q < next_lanes, nxt, jnp.where(lane_q == 2, up, down)))
    xx = shifted - h

    def mix(p):
        return (h + xx * mu_ref[p:p + 1, :]).astype(BF16)

    vec = vec_ref[...]
    r = jnp.dot(mix(0), wm_ref[0], preferred_element_type=F32)
    k = jnp.dot(mix(2), wm_ref[1], preferred_element_type=F32)
    v = jnp.dot(mix(3), wm_ref[2], preferred_element_type=F32)
    z_ref[...] = jnp.dot(mix(5), wm_ref[3], preferred_element_type=F32).astype(BF16)
    tw = jnp.tanh(jnp.dot(mix(1), w1_ref[...], preferred_element_type=F32))
    ta = jnp.dot(mix(4), a1_ref[...], preferred_element_type=F32)

    kk = k * vec[4:5]
    kk = kk / jnp.maximum(jnp.sqrt(_head_sum(kk * kk, e_ref)), 1e-12)
    r_ref[...] = r
    v_ref[...] = v
    kk_ref[...] = kk
    bonus = jnp.zeros_like(r)
    for d in range(2):
        lw = vec[d:d + 1] + _bdot(tw, w2_ref[d])
        w_log = -(jnp.maximum(-lw, 0.0) + jnp.log(1.0 + jnp.exp(-jnp.abs(lw)))) - 0.5
        ld_ref[d] = -jnp.exp(w_log)
        a = 1.0 / (1.0 + jnp.exp(-(vec[2 + d:3 + d] + _bdot(ta, a2_ref[d]))))
        kd = k * (1.0 + (a - 1.0) * vec[5:6])
        k_ref[d] = kd
        b_ref[d] = kk * a
        bonus = bonus + r * kd * vec[6:7]
    bonus_ref[...] = _head_sum(bonus, e_ref) * v


def _rw_prepare(xc, mod4, g, mu, wm, w1, w2, a1, a2, vec, e, layer, grid_rows):
    n_batch, lc, _ = xc.shape
    tiles = lc // ROW_TILE
    per_tile = ROW_TILE // GRID_W
    last = lc // GRID_W - 1

    def full(a):
        return pl.BlockSpec(a.shape, lambda b, t: (0,) * a.ndim)

    tile_spec = pl.BlockSpec((None, ROW_TILE, D), lambda b, t: (b, t, 0))
    dir_spec = pl.BlockSpec((2, None, ROW_TILE, D), lambda b, t: (0, b, t, 0))
    one = jax.ShapeDtypeStruct((n_batch, lc, D), F32)
    two = jax.ShapeDtypeStruct((2, n_batch, lc, D), F32)
    return pl.pallas_call(
        functools.partial(_rw_prep_kernel, grid_rows=grid_rows),
        grid=(n_batch, tiles),
        in_specs=[pl.BlockSpec((None, GRID_W, D), lambda b, t: (b, jnp.maximum(t * per_tile - 1, 0), 0)),
                  tile_spec,
                  pl.BlockSpec((None, GRID_W, D), lambda b, t: (b, jnp.minimum((t + 1) * per_tile, last), 0)),
                  _mod_spec(layer, n_batch, _identity), full(g), full(mu), full(wm), full(w1), full(w2),
                  full(a1), full(a2), full(vec), full(e)],
        out_specs=[tile_spec, tile_spec, tile_spec, dir_spec, dir_spec, dir_spec, tile_spec, tile_spec],
        out_shape=[one, one, one, two, two, two, one, jax.ShapeDtypeStruct((n_batch, lc, D), BF16)],
        compiler_params=_params("parallel", "parallel"),
    )(xc, xc, xc, mod4, g, mu, wm, w1, w2, a1, a2, vec, e)


def _wkv_masks(reverse):
    c, hg = WKV_CHUNK, WKV_GROUP
    t = lax.broadcasted_iota(jnp.int32, (c, hg * c), 0)
    s = lax.broadcasted_iota(jnp.int32, (c, hg * c), 1) % c
    strict = jnp.where((s > t) if reverse else (s < t), 1.0, 0.0)
    incl = jnp.where((s >= t) if reverse else (s <= t), 1.0, 0.0)
    eye = jnp.where(s == t, 1.0, 0.0)
    return strict, jnp.concatenate([incl, incl], axis=1), eye, jnp.concatenate([incl[:, :c]] * 3, axis=1).astype(BF16)


def _wkv_chunks(chains, m4, bd, bd_state):
    c, w, hg = WKV_CHUNK, WKV_LANES, WKV_GROUP
    n = range(len(chains))
    refs, rows, masks, reverse = zip(*[(ch["refs"], (ch["rows"], ch["lanes"]), ch["masks"], ch["reverse"])
                                       for ch in chains])
    strict, incl2, eye, cum_mat = zip(*masks)

    def dot(a, b, dims=None):
        if dims is None:
            return jnp.dot(a, b, preferred_element_type=F32)
        return lax.dot_general(a, b, dims, preferred_element_type=F32)

    def stack(x):
        return jnp.where(jnp.concatenate([m4] * (x.shape[1] // w), axis=1), jnp.concatenate([x] * hg, axis=0), 0)

    def blockdiag(x):
        return jnp.where(bd, jnp.concatenate([x.astype(BF16)] * hg, axis=0), 0)

    ld = [refs[i][5][rows[i]] for i in n]
    cum = [dot(cum_mat[i], jnp.concatenate(_split3(ld[i]), axis=0)) for i in n]
    tot = [cum[i][0:1] if reverse[i] else cum[i][c - 1:c] for i in n]
    e_out = [jnp.exp(-cum[i]) for i in n]
    vb = [refs[i][4][rows[i]].astype(BF16) for i in n]
    rt = [(refs[i][0][rows[i]] * jnp.exp(cum[i])).astype(BF16) for i in n]
    at = [(-refs[i][1][rows[i]] * jnp.exp(cum[i] - ld[i])).astype(BF16) for i in n]
    kt = [(refs[i][3][rows[i]] * e_out[i]).astype(BF16) for i in n]
    bt = [(refs[i][2][rows[i]] * e_out[i]).astype(BF16) for i in n]

    sc = [dot(jnp.concatenate([at[i], rt[i]], axis=0), jnp.concatenate([stack(kt[i]), stack(bt[i])], axis=0), _NT)
          for i in n]
    a_ak = [(sc[i][:c, :w] * strict[i]).astype(BF16) for i in n]
    p = [sc[i][:c, w:] * strict[i] for i in n]
    a_r = [(sc[i][c:] * incl2[i]).astype(BF16) for i in n]
    tinv = [eye[i] + p[i] for i in n]
    for _ in range(c.bit_length() - 2):
        p = [dot(p[i].astype(BF16), blockdiag(p[i])) for i in n]
        tinv = [tinv[i] + dot(p[i].astype(BF16), blockdiag(tinv[i])) for i in n]
    sv = [stack(vb[i]) for i in n]
    akv = [dot(a_ak[i], sv[i]).astype(BF16) for i in n]
    tx = [dot(tinv[i].astype(BF16), stack(jnp.concatenate([at[i], akv[i]], axis=1))) for i in n]

    ht = [ch["ht"][ch["slot"]] for ch in chains]
    wrh = [dot(jnp.concatenate([tx[i][:, :w].astype(BF16), rt[i]], axis=0), ht[i].astype(BF16), _NT) for i in n]
    ub = [(wrh[i][:c] + tx[i][:, w:]).astype(BF16) for i in n]
    y = [wrh[i][c:] + dot(a_r[i], jnp.concatenate([sv[i], stack(ub[i])], axis=0)) for i in n]
    e_end = [jnp.exp(tot[i] - cum[i]) for i in n]
    kb = [jnp.concatenate([(refs[i][3][rows[i]] * e_end[i]).astype(BF16),
                           (refs[i][2][rows[i]] * e_end[i]).astype(BF16)], axis=0) for i in n]
    upd = [dot(jnp.concatenate([vb[i], ub[i]], axis=0), kb[i], _TN) for i in n]
    for i, ch in enumerate(chains):
        ch["y"][rows[i]] = y[i]
        ch["ht"][ch["slot"]] = jnp.where(bd_state, ht[i] * jnp.exp(tot[i]) + upd[i], 0.0)


def _wkv_kernel(rf, kkf, bf, kf, vf, ldf, rb, kkb, bb, kb, vb, ldb, yf_ref, yb_ref, ht_ref):
    @pl.when(pl.program_id(1) == 0)
    def _():
        ht_ref[...] = jnp.zeros_like(ht_ref)

    c, w, hg = WKV_CHUNK, WKV_LANES, WKV_GROUP
    n_chunks = ROW_TILE // c
    lane_head = lax.broadcasted_iota(jnp.int32, (hg * c, w), 1) // RW_HEAD
    row_head = lax.broadcasted_iota(jnp.int32, (hg * c, w), 0) // c
    m4 = lane_head == row_head
    bd = lax.broadcasted_iota(jnp.int32, (hg * c, hg * c), 1) // c == row_head
    bd_state = (lax.broadcasted_iota(jnp.int32, (w, w), 0) // RW_HEAD
                == lax.broadcasted_iota(jnp.int32, (w, w), 1) // RW_HEAD)
    masks_f, masks_b = _wkv_masks(False), _wkv_masks(True)
    groups = D // w

    def body(step, carry):
        rows_f = pl.ds(pl.multiple_of(step * c, c), c)
        rows_b = pl.ds(pl.multiple_of((n_chunks - 1 - step) * c, c), c)
        chains = []
        for g in range(groups):
            lanes = slice(g * w, (g + 1) * w)
            chains.append(dict(refs=(rf, kkf, bf, kf, vf, ldf), rows=rows_f, lanes=lanes, masks=masks_f,
                               reverse=False, ht=ht_ref, slot=g, y=yf_ref))
            chains.append(dict(refs=(rb, kkb, bb, kb, vb, ldb), rows=rows_b, lanes=lanes, masks=masks_b,
                               reverse=True, ht=ht_ref, slot=groups + g, y=yb_ref))
        _wkv_chunks(chains, m4, bd, bd_state)
        return carry

    lax.fori_loop(0, n_chunks, body, 0)


def _wkv(r, kk, b, k, v, ld):
    n_batch, lc, _ = r.shape
    tiles = lc // ROW_TILE
    bt = _bwd_tile(tiles)

    def specs(tile_of, direction):
        one = pl.BlockSpec((None, ROW_TILE, D), lambda bb, s: (bb, tile_of(s), 0))
        two = pl.BlockSpec((None, None, ROW_TILE, D), lambda bb, s: (direction, bb, tile_of(s), 0))
        return one, two

    f1, f2 = specs(_identity, 0)
    b1, b2 = specs(bt, 1)
    return pl.pallas_call(
        _wkv_kernel,
        grid=(n_batch, tiles),
        in_specs=[f1, f1, f2, f2, f1, f2, b1, b1, b2, b2, b1, b2],
        out_specs=[f1, b1],
        out_shape=[jax.ShapeDtypeStruct(r.shape, F32)] * 2,
        scratch_shapes=[pltpu.VMEM((2 * (D // WKV_LANES), WKV_LANES, WKV_LANES), F32)],
        compiler_params=_params("parallel", "arbitrary"),
    )(r, kk, b, k, v, ld, r, kk, b, k, v, ld)


def _rw_out_kernel(yf_ref, yb_ref, bonus_ref, z_ref, x_ref, mod_ref, vec_ref, e_ref, wo_ref, xo_ref):
    y = yf_ref[...] + yb_ref[...]
    inv = 1.0 / RW_HEAD
    yc = y - _head_sum(y, e_ref) * inv
    yn = yc * lax.rsqrt(_head_sum(yc * yc, e_ref) * inv + RW_LNX_EPS)
    yn = yn * vec_ref[0:1] + vec_ref[1:2]
    o = (yn + bonus_ref[...]) * _silu(z_ref[...].astype(F32))
    xo_ref[...] = x_ref[...] + mod_ref[:, 2 * D:] * _bdot(o, wo_ref[...])


def _rw_output(xc, y_f, y_b, bonus, z, mod4, vec, e, w_out, layer):
    n_batch, lc, _ = xc.shape
    tiles = lc // ROW_TILE
    tile_spec = pl.BlockSpec((None, ROW_TILE, D), lambda b, t: (b, t, 0))

    def full(a):
        return pl.BlockSpec(a.shape, lambda b, t: (0,) * a.ndim)

    return pl.pallas_call(
        _rw_out_kernel,
        grid=(n_batch, tiles),
        in_specs=[tile_spec, tile_spec, tile_spec, tile_spec, tile_spec, _mod_spec(layer, n_batch, _identity),
                  full(vec), full(e), full(w_out)],
        out_specs=tile_spec,
        out_shape=jax.ShapeDtypeStruct(xc.shape, F32),
        compiler_params=_params("parallel", "parallel"),
    )(y_f, y_b, bonus, z, xc, mod4, vec, e, w_out)


def _rwkv(xc, mod4, g, mu, w_rkvg, w0, w1, w2, a0, a1, a2, k_k, k_a, r_k, lnx_g, lnx_b, w_out, layer, grid_rows):
    zeros = jnp.zeros((RW_LORA, D), F32)
    w1c = jnp.concatenate([w1[0], w1[1]], axis=1).astype(BF16)
    a1c = jnp.concatenate([a1[0], a1[1]], axis=1).astype(BF16)
    w2p = jnp.stack([jnp.concatenate([w2[0], zeros]), jnp.concatenate([zeros, w2[1]])]).astype(BF16)
    a2p = jnp.stack([jnp.concatenate([a2[0], zeros]), jnp.concatenate([zeros, a2[1]])]).astype(BF16)
    zrow = jnp.zeros((D,), F32)
    vec = jnp.stack([w0[0], w0[1], a0[0], a0[1], k_k, k_a, r_k.reshape(D), zrow])
    lane = jnp.arange(WKV_LANES) // RW_HEAD
    e = (lane[:, None] == lane[None, :]).astype(BF16)
    r, v, kk, k, b, ld, bonus, z = _rw_prepare(xc, mod4, g.reshape(1, D), mu, w_rkvg.astype(BF16), w1c, w2p,
                                               a1c, a2p, vec, e, layer, grid_rows)
    y_f, y_b = _wkv(r, kk, b, k, v, ld)
    vec_o = jnp.stack([lnx_g, lnx_b] + [zrow] * 6)
    return _rw_output(xc, y_f, y_b, bonus, z, mod4, vec_o, e, w_out.astype(BF16), layer)


def _final_kernel(x_ref, g_ref, o_ref):
    x = x_ref[...]
    o_ref[...] = x * lax.rsqrt(jnp.mean(x * x, axis=-1, keepdims=True) + NORM_EPS) * g_ref[...]


def _final_norm(xc, g, ctx_tiles):
    n_batch, lc, _ = xc.shape
    tiles = lc // ROW_TILE - ctx_tiles
    return pl.pallas_call(
        _final_kernel,
        grid=(n_batch, tiles),
        in_specs=[pl.BlockSpec((None, ROW_TILE, D), lambda b, t: (b, t + ctx_tiles, 0)),
                  pl.BlockSpec((1, D), lambda b, t: (0, 0))],
        out_specs=pl.BlockSpec((None, ROW_TILE, D), lambda b, t: (b, t, 0)),
        out_shape=jax.ShapeDtypeStruct((n_batch, tiles * ROW_TILE, D), F32),
        compiler_params=_params("parallel", "parallel"),
    )(xc, g.reshape(1, D))


def kernel(x, c, ctx, c_ctx, ada_w, ada_b, norm_g, final_g, ret_w_in, ret_decay, ret_w_out, gm_w_in, gm_vnorm_g,
           gm_w_s, gm_b_s, gm_w_out, rw_mu, rw_w_rkvg, rw_w0, rw_w1, rw_w2, rw_a0, rw_a1, rw_a2, rw_k_k, rw_k_a,
           rw_r_k, rw_lnx_g, rw_lnx_b, rw_w_out):
    n_batch, seq_len, _ = x.shape
    ctx_len = ctx.shape[1]
    depth = ada_w.shape[0]
    assert ctx_len == ROW_TILE and seq_len % ROW_TILE == 0 and n_batch < 8

    xc = jnp.concatenate([ctx, x], axis=1)
    cvec = jnp.zeros((8, D), F32).at[:n_batch].set(c).at[n_batch].set(c_ctx)
    mod4 = _modulation(cvec, ada_w, ada_b).reshape(depth, 8, 1, 3 * D)
    rope = _rope_table(ctx_len, seq_len)
    k_scale = jnp.concatenate([jnp.ones((RET_HEADS * RET_DK,), F32),
                               jnp.full((RET_HEADS * RET_DK,), RET_DK ** -0.5, F32),
                               jnp.ones((2 * RET_HEADS * RET_DV,), F32)])

    for i in range(depth):
        kind, j = i % N_MIXERS, i // N_MIXERS
        if kind == 0:
            w_in = (ret_w_in[j] * k_scale).astype(BF16)
            proj = _project(xc, mod4, norm_g[i], w_in, rope, i, 2 * RET_HEADS * RET_DK)
            tab, cd = _ret_tables(ret_decay[j])
            xc = _retention(xc, proj, mod4, ret_w_out[j].astype(BF16), tab, cd, i)
        elif kind == 1:
            proj = _project(xc, mod4, norm_g[i], gm_w_in[j].astype(BF16), rope, i, 0)
            xc = _gmlp(xc, proj, mod4, gm_vnorm_g[j].reshape(1, GM_WIDTH), gm_w_s[j].astype(BF16),
                       gm_b_s[j].T, gm_w_out[j].astype(BF16), i)
        else:
            xc = _rwkv(xc, mod4, norm_g[i], rw_mu[j], rw_w_rkvg[j], rw_w0[j], rw_w1[j], rw_w2[j], rw_a0[j],
                       rw_a1[j], rw_a2[j], rw_k_k[j], rw_k_a[j], rw_r_k[j], rw_lnx_g[j], rw_lnx_b[j],
                       rw_w_out[j], i, seq_len // GRID_W)
    return _final_norm(xc, final_g, ctx_len // ROW_TILE)
```

```python
import functools
import math

import jax
import jax.numpy as jnp
from jax import lax
from jax.experimental import pallas as pl
from jax.experimental.pallas import tpu as pltpu

F32 = jnp.float32
BF16 = jnp.bfloat16

D = 1024
GRID_W = 64
NORM_EPS = 1e-6
N_MIXERS = 3
ROW_TILE = 256
RET_HEADS = 4
RET_DK = 256
RET_DV = 512
ROPE_BASE = 10000.0
GM_WIDTH = 2 * D
GM_GROUPS = 8
GM_CHUNK = 128
RW_HEAD = 64
RW_LORA = 64
RW_LNX_EPS = 64e-5
WKV_CHUNK = 64
WKV_LANES = 256
WKV_GROUP = WKV_LANES // RW_HEAD
VMEM_LIMIT = 56 * 1024 * 1024

_NT = (((1,), (1,)), ((), ()))
_TN = (((0,), (0,)), ((), ()))


def _params(*sem):
    return pltpu.CompilerParams(dimension_semantics=sem, vmem_limit_bytes=VMEM_LIMIT)


def _silu(x):
    return x / (1.0 + jnp.exp(-x))


def _bdot(a, b, dims=None):
    a = a.astype(BF16)
    b = b.astype(BF16)
    if dims is None:
        return jnp.dot(a, b, preferred_element_type=F32)
    return lax.dot_general(a, b, dims, preferred_element_type=F32)


def _split3(x):
    hi = x.astype(BF16)
    r1 = x - hi.astype(F32)
    mid = r1.astype(BF16)
    lo = (r1 - mid.astype(F32)).astype(BF16)
    return hi, mid, lo


def _norm_mod(x, g, mod):
    ms = jnp.mean(x * x, axis=-1, keepdims=True)
    y = x * lax.rsqrt(ms + NORM_EPS) * g
    return y * (1.0 + mod[:, D:2 * D]) + mod[:, :D]


def _mod_spec(layer, n_batch, tile_of_step):
    def index(b, s):
        t = tile_of_step(s)
        return (layer, jnp.where(t == 0, n_batch, b), 0, 0)
    return pl.BlockSpec((None, None, 1, 3 * D), index)


def _identity(s):
    return s


def _mod_kernel(c_ref, w_ref, b_ref, o_ref):
    s = _silu(c_ref[...])
    o_ref[...] = jnp.dot(s, w_ref[...], preferred_element_type=F32,
                         precision=lax.Precision.HIGHEST) + b_ref[...]


def _modulation(cvec, ada_w, ada_b):
    depth = ada_w.shape[0]
    return pl.pallas_call(
        _mod_kernel,
        grid=(depth, 3),
        in_specs=[pl.BlockSpec((8, D), lambda i, j: (0, 0)),
                  pl.BlockSpec((None, D, D), lambda i, j: (i, 0, j)),
                  pl.BlockSpec((None, 1, D), lambda i, j: (i, 0, j))],
        out_specs=pl.BlockSpec((None, 8, D), lambda i, j: (i, 0, j)),
        out_shape=jax.ShapeDtypeStruct((depth, 8, 3 * D), F32),
        compiler_params=_params("arbitrary", "arbitrary"),
    )(cvec, ada_w, ada_b.reshape(depth, 1, 3 * D))


PROJ_COLS = 512


def _ret_proj_kernel(x_ref, mod_ref, g_ref, w_ref, cs_ref, o_ref):
    h = _norm_mod(x_ref[...], g_ref[...], mod_ref[...]).astype(BF16)
    half = RET_DK // 2
    qk, zc = 2 * RET_HEADS * RET_DK, 2 * RET_HEADS * RET_DK + RET_HEADS * RET_DV
    cos = cs_ref[:, :half]
    sin = cs_ref[:, half:]
    for c0 in range(0, qk, RET_DK):
        acc = jnp.dot(h, w_ref[:, c0:c0 + RET_DK], preferred_element_type=F32)
        t1, t2 = acc[:, :half], acc[:, half:]
        o_ref[:, c0:c0 + half] = (t1 * cos - t2 * sin).astype(BF16)
        o_ref[:, c0 + half:c0 + RET_DK] = (t1 * sin + t2 * cos).astype(BF16)
    for c0 in range(qk, o_ref.shape[-1], PROJ_COLS):
        acc = jnp.dot(h, w_ref[:, c0:c0 + PROJ_COLS], preferred_element_type=F32)
        o_ref[:, c0:c0 + PROJ_COLS] = (_silu(acc) if c0 >= zc else acc).astype(BF16)


def _gm_proj_kernel(x_ref, mod_ref, g_ref, w_ref, vg_ref, o_ref, v_ref):
    h = _norm_mod(x_ref[...], g_ref[...], mod_ref[...]).astype(BF16)
    for c0 in range(GM_WIDTH, 2 * GM_WIDTH, PROJ_COLS):
        v_ref[:, c0 - GM_WIDTH:c0 - GM_WIDTH + PROJ_COLS] = jnp.dot(h, w_ref[:, c0:c0 + PROJ_COLS],
                                                                    preferred_element_type=F32)
    v = v_ref[...]
    v = v - jnp.mean(v, axis=-1, keepdims=True)
    vn = v * lax.rsqrt(jnp.mean(v * v, axis=-1, keepdims=True) + NORM_EPS) * vg_ref[...]
    o_ref[:, GM_WIDTH:2 * GM_WIDTH] = vn.astype(BF16)
    for c0 in range(0, GM_WIDTH, PROJ_COLS):
        o_ref[:, c0:c0 + PROJ_COLS] = jnp.dot(h, w_ref[:, c0:c0 + PROJ_COLS], preferred_element_type=F32).astype(BF16)
    for c0 in range(2 * GM_WIDTH, 3 * GM_WIDTH, PROJ_COLS):
        acc = jnp.dot(h, w_ref[:, c0:c0 + PROJ_COLS], preferred_element_type=F32)
        o_ref[:, c0:c0 + PROJ_COLS] = _silu(acc).astype(BF16)


def _project(body, xc, mod4, g, w, aux, aux_spec, layer, scratch=()):
    n_batch, lc, _ = xc.shape
    n_out = w.shape[1]
    tiles = lc // ROW_TILE
    return pl.pallas_call(
        body,
        grid=(n_batch, tiles),
        in_specs=[pl.BlockSpec((None, ROW_TILE, D), lambda b, t: (b, t, 0)),
                  _mod_spec(layer, n_batch, _identity),
                  pl.BlockSpec((1, D), lambda b, t: (0, 0)),
                  pl.BlockSpec((D, n_out), lambda b, t: (0, 0)),
                  aux_spec],
        out_specs=pl.BlockSpec((None, ROW_TILE, n_out), lambda b, t: (b, t, 0)),
        out_shape=jax.ShapeDtypeStruct((n_batch, lc, n_out), BF16),
        scratch_shapes=list(scratch),
        compiler_params=_params("parallel", "parallel"),
    )(xc, mod4, g.reshape(1, D), w, aux)


def _rope_table(ctx_len, seq_len):
    n_freq = RET_DK // 4
    freqs = ROPE_BASE ** (-jnp.arange(n_freq, dtype=F32) / n_freq)
    t = jnp.arange(seq_len, dtype=jnp.int32)
    row, col = (t // GRID_W).astype(F32), (t % GRID_W).astype(F32)
    ang = jnp.concatenate([row[:, None] * freqs, col[:, None] * freqs], axis=-1)
    ang = jnp.concatenate([jnp.zeros((ctx_len, RET_DK // 2), F32), ang], axis=0)
    return jnp.concatenate([jnp.cos(ang), jnp.sin(ang)], axis=-1)


RET_ROW_LANES = 128


def _ret_table_kernel(dl_ref, mask_ref, row_ref, cd_ref):
    c = ROW_TILE
    i = lax.broadcasted_iota(jnp.int32, (c, c), 0).astype(F32)
    j = lax.broadcasted_iota(jnp.int32, (c, c), 1).astype(F32)
    ir = i[:, :RET_ROW_LANES]

    def log_sigmoid(x):
        return jnp.minimum(x, 0.0) - jnp.log(1.0 + jnp.exp(-jnp.abs(x)))

    for h in range(RET_HEADS):
        lg_f = log_sigmoid(jnp.full((c, c), dl_ref[0, h], F32))
        lg_b = log_sigmoid(jnp.full((c, c), dl_ref[1, h], F32))
        fwd = jnp.where(i >= j, jnp.exp(jnp.maximum(i - j, 0.0) * lg_f), 0.0)
        bwd = jnp.where(j >= i, jnp.exp(jnp.maximum(j - i, 0.0) * lg_b), 0.0)
        mask_ref[h] = fwd + bwd
        lf = log_sigmoid(jnp.full((c, RET_ROW_LANES), dl_ref[0, h], F32))
        lb = log_sigmoid(jnp.full((c, RET_ROW_LANES), dl_ref[1, h], F32))
        row_ref[h, 0] = jnp.exp((ir + 1.0) * lf)
        row_ref[h, 1] = jnp.exp((c - 1.0 - ir) * lf)
        row_ref[h, 2] = jnp.exp((c - ir) * lb)
        row_ref[h, 3] = jnp.exp(ir * lb)
        cd_ref[h, 0] = jnp.exp(c * log_sigmoid(jnp.full((8, RET_DV), dl_ref[0, h], F32)))
        cd_ref[h, 1] = jnp.exp(c * log_sigmoid(jnp.full((8, RET_DV), dl_ref[1, h], F32)))


def _ret_tables(decay_logit):
    c = ROW_TILE
    return pl.pallas_call(
        _ret_table_kernel,
        in_specs=[pl.BlockSpec(memory_space=pltpu.SMEM)],
        out_shape=(jax.ShapeDtypeStruct((RET_HEADS, c, c), F32),
                   jax.ShapeDtypeStruct((RET_HEADS, 4, c, RET_ROW_LANES), F32),
                   jax.ShapeDtypeStruct((RET_HEADS, 2, 8, RET_DV), F32)),
    )(decay_logit)


def _lanes(t, width):
    return jnp.concatenate([t] * (width // RET_ROW_LANES), axis=1)


def _ret_heads(q_ref, k_ref, v_ref):
    return [(q_ref[:, h * RET_DK:(h + 1) * RET_DK], k_ref[:, h * RET_DK:(h + 1) * RET_DK],
             v_ref[:, h * RET_DV:(h + 1) * RET_DV]) for h in range(RET_HEADS)]


def _ret_fwd_kernel(q_ref, k_ref, v_ref, mask_ref, row_ref, cd_ref, o_ref, s_ref):
    @pl.when(pl.program_id(1) == 0)
    def _():
        s_ref[...] = jnp.zeros_like(s_ref)

    qkv = _ret_heads(q_ref, k_ref, v_ref)
    sc = {}

    def scores(h):
        q, k, _ = qkv[h]
        sc[h] = lax.dot_general(q, k, _NT, preferred_element_type=F32)

    def finish(h):
        q, k, v = qkv[h]
        s = s_ref[h]
        qs = _bdot(q, s)
        kd = k.astype(F32) * _lanes(row_ref[h, 1], RET_DK)
        s_ref[h] = s * cd_ref[h, 0, 0:1, :] + _bdot(kd, v, _TN)
        o = _bdot(sc[h] * mask_ref[h], v)
        o_ref[:, h * RET_DV:(h + 1) * RET_DV] = o + qs * _lanes(row_ref[h, 0], RET_DV)

    scores(0)
    for h in range(RET_HEADS):
        if h + 1 < RET_HEADS:
            scores(h + 1)
        finish(h)


def _ret_bwd_kernel(q_ref, k_ref, v_ref, z_ref, op_ref, x_ref, mod_ref, wo_ref, row_ref, cd_ref, fg_ref,
                    xo_ref, s_ref, *, final):
    @pl.when(pl.program_id(1) == 0)
    def _():
        s_ref[...] = jnp.zeros_like(s_ref)

    qkv = _ret_heads(q_ref, k_ref, v_ref)
    gz = {}

    def inter(h):
        q, k, v = qkv[h]
        vv = slice(h * RET_DV, (h + 1) * RET_DV)
        s = s_ref[h]
        o = op_ref[:, vv] + _bdot(q, s) * _lanes(row_ref[h, 2], RET_DV)
        kd = k.astype(F32) * _lanes(row_ref[h, 3], RET_DK)
        s_ref[h] = s * cd_ref[h, 1, 0:1, :] + _bdot(kd, v, _TN)
        o = o * lax.rsqrt(jnp.mean(o * o, axis=-1, keepdims=True) + NORM_EPS)
        gz[h] = (o * z_ref[:, vv].astype(F32)).astype(BF16)

    acc = jnp.zeros((ROW_TILE, D), F32)
    inter(0)
    for h in range(RET_HEADS):
        if h + 1 < RET_HEADS:
            inter(h + 1)
        acc = acc + jnp.dot(gz[h], wo_ref[h * RET_DV:(h + 1) * RET_DV, :], preferred_element_type=F32)
    xn = x_ref[...] + mod_ref[:, 2 * D:] * acc
    if final:
        xn = xn * lax.rsqrt(jnp.mean(xn * xn, axis=-1, keepdims=True) + NORM_EPS) * fg_ref[...]
    xo_ref[...] = xn


def _bwd_tile(tiles):
    return lambda s: jnp.where(s == 0, 0, tiles - s)


def _retention(xc, proj, mod4, w_out, tables, layer, final_g):
    n_batch, lc, _ = xc.shape
    tiles = lc // ROW_TILE
    width = RET_HEADS * RET_DV
    state = pltpu.VMEM((RET_HEADS, RET_DK, RET_DV), F32)
    mask, row, cd = tables
    mask_spec = pl.BlockSpec(mask.shape, lambda b, s: (0, 0, 0))
    row_spec = pl.BlockSpec(row.shape, lambda b, s: (0, 0, 0, 0))
    cd_spec = pl.BlockSpec(cd.shape, lambda b, s: (0, 0, 0, 0))

    def col_spec(width_, col, tile_of_step):
        return pl.BlockSpec((None, ROW_TILE, width_), lambda b, s: (b, tile_of_step(s), col))

    o_part = pl.pallas_call(
        _ret_fwd_kernel,
        grid=(n_batch, tiles),
        in_specs=[col_spec(RET_HEADS * RET_DK, 0, _identity), col_spec(RET_HEADS * RET_DK, 1, _identity),
                  col_spec(width, 1, _identity), mask_spec, row_spec, cd_spec],
        out_specs=col_spec(width, 0, _identity),
        out_shape=jax.ShapeDtypeStruct((n_batch, lc, width), F32),
        scratch_shapes=[state],
        compiler_params=_params("parallel", "arbitrary"),
    )(proj, proj, proj, mask, row, cd)

    bt = _bwd_tile(tiles)
    final = final_g is not None
    if final:
        ctx_tiles = 1
        out_spec = col_spec(D, 0, lambda s: bt(jnp.maximum(s, 1)) - ctx_tiles)
        out_shape = jax.ShapeDtypeStruct((n_batch, lc - ctx_tiles * ROW_TILE, D), F32)
        fg = final_g.reshape(1, D)
    else:
        out_spec = col_spec(D, 0, bt)
        out_shape = jax.ShapeDtypeStruct(xc.shape, F32)
        fg = jnp.ones((1, D), F32)
    return pl.pallas_call(
        functools.partial(_ret_bwd_kernel, final=final),
        grid=(n_batch, tiles),
        in_specs=[col_spec(RET_HEADS * RET_DK, 0, bt), col_spec(RET_HEADS * RET_DK, 1, bt),
                  col_spec(width, 1, bt), col_spec(width, 2, bt), col_spec(width, 0, bt),
                  col_spec(D, 0, bt), _mod_spec(layer, n_batch, bt),
                  pl.BlockSpec((width, D), lambda b, s: (0, 0)), row_spec, cd_spec,
                  pl.BlockSpec((1, D), lambda b, s: (0, 0))],
        out_specs=out_spec,
        out_shape=out_shape,
        scratch_shapes=[state],
        compiler_params=_params("parallel", "arbitrary"),
    )(proj, proj, proj, proj, o_part, xc, mod4, w_out, row, cd, fg)


def _gm_kernel(u_ref, v_ref, z_ref, x_ref, mod_ref, ws_ref, bs_ref, wo_ref, xo_ref):
    c = GM_CHUNK
    gw = GM_WIDTH // GM_GROUPS
    mixed = {}

    def mix(g):
        cols = slice(g * gw, (g + 1) * gw)
        mixed[g] = jnp.concatenate(
            [jnp.dot(ws_ref[g], v_ref[ci * c:(ci + 1) * c, cols], preferred_element_type=F32)
             for ci in range(ROW_TILE // c)], axis=0) + jnp.concatenate([bs_ref[:, g:g + 1]] * (ROW_TILE // c), axis=0)

    acc = jnp.zeros((ROW_TILE, D), F32)
    mix(0)
    for g in range(GM_GROUPS):
        if g + 1 < GM_GROUPS:
            mix(g + 1)
        cols = slice(g * gw, (g + 1) * gw)
        y = u_ref[:, cols].astype(F32) * mixed[g] * z_ref[:, cols].astype(F32)
        acc = acc + _bdot(y, wo_ref[cols, :])
    xo_ref[...] = x_ref[...] + mod_ref[:, 2 * D:] * acc


def _gmlp(xc, proj, mod4, w_s, b_s, w_out, layer):
    n_batch, lc, _ = xc.shape
    tiles = lc // ROW_TILE

    def col_spec(width, col):
        return pl.BlockSpec((None, ROW_TILE, width), lambda b, t: (b, t, col))

    def full(a):
        return pl.BlockSpec(a.shape, lambda b, t: (0,) * a.ndim)

    return pl.pallas_call(
        _gm_kernel,
        grid=(n_batch, tiles),
        in_specs=[col_spec(GM_WIDTH, 0), col_spec(GM_WIDTH, 1), col_spec(GM_WIDTH, 2), col_spec(D, 0),
                  _mod_spec(layer, n_batch, _identity), full(w_s), full(b_s), full(w_out)],
        out_specs=col_spec(D, 0),
        out_shape=jax.ShapeDtypeStruct(xc.shape, F32),
        compiler_params=_params("parallel", "parallel"),
    )(proj, proj, proj, xc, mod4, w_s, b_s, w_out)


def _head_sum(x, e_ref):
    outs = []
    e = e_ref[...]
    for g in range(x.shape[-1] // WKV_LANES):
        xs = x[:, g * WKV_LANES:(g + 1) * WKV_LANES]
        hi = xs.astype(BF16)
        lo = (xs - hi.astype(F32)).astype(BF16)
        outs.append(jnp.dot(hi, e, preferred_element_type=F32) + jnp.dot(lo, e, preferred_element_type=F32))
    return jnp.concatenate(outs, axis=1)


def _rw_prep_kernel(xp_ref, x_ref, xn_ref, mod_ref, g_ref, mu_ref, wm_ref, w1_ref, w2_ref, a1_ref, a2_ref,
                    vec_ref, e_ref, r_ref, v_ref, kk_ref, k_ref, b_ref, ld_ref, bonus_ref, z_ref,
                    *, grid_rows):
    t = pl.program_id(1)
    is_ctx = t == 0
    g, mod = g_ref[...], mod_ref[...]
    h = _norm_mod(x_ref[...], g, mod)
    h_up = _norm_mod(xp_ref[...], g, mod)
    h_dn = _norm_mod(xn_ref[...], g, mod)
    n = ROW_TILE
    prev = pltpu.roll(h, 1, 0)
    nxt = pltpu.roll(h, n - 1, 0)
    ext = jnp.concatenate([h_up, h, h_dn], axis=0)
    up, down = ext[:n], ext[2 * GRID_W:]

    lane_q = lax.broadcasted_iota(jnp.int32, (n, D), 1) // (D // 4)
    i = lax.broadcasted_iota(jnp.int32, (n, D), 0)
    grow = (t - 1) * (n // GRID_W) + i // GRID_W
    pos = jnp.where(is_ctx, i, i % GRID_W)
    last_pos = jnp.where(is_ctx, n - 1, GRID_W - 1)
    prev_lanes = jnp.where(is_ctx, 2, 1)
    next_lanes = jnp.where(is_ctx, 4, 2)
    prev = jnp.where(pos > 0, prev, 0.0)
    nxt = jnp.where(pos < last_pos, nxt, 0.0)
    up = jnp.where(grow > 0, up, 0.0)
    down = jnp.where(grow < grid_rows - 1, down, 0.0)
    shifted = jnp.where(lane_q < prev_lanes, prev,
                        jnp.where(lane_q < next_lanes, nxt, jnp.where(lane_q == 2, up, down)))
    xx = shifted - h

    hb, xb = h.astype(BF16), xx.astype(BF16)

    def mix(p):
        return hb + xb * mu_ref[p:p + 1, :].astype(BF16)

    vec = vec_ref[...]
    r = jnp.dot(mix(0), wm_ref[0], preferred_element_type=F32)
    k = jnp.dot(mix(2), wm_ref[1], preferred_element_type=F32)
    v = jnp.dot(mix(3), wm_ref[2], preferred_element_type=F32)
    z_ref[...] = jnp.dot(mix(5), wm_ref[3], preferred_element_type=F32).astype(BF16)
    tw = jnp.tanh(jnp.dot(mix(1), w1_ref[...], preferred_element_type=F32))
    ta = jnp.dot(mix(4), a1_ref[...], preferred_element_type=F32)

    kk = k * vec[4:5]
    kk = kk / jnp.maximum(jnp.sqrt(_head_sum(kk * kk, e_ref)), 1e-12)
    r_ref[...] = r
    v_ref[...] = v
    kk_ref[...] = kk
    bonus = jnp.zeros_like(r)
    for d in range(2):
        lw = vec[d:d + 1] + _bdot(tw, w2_ref[d])
        ld_ref[d] = -math.exp(-0.5) / (1.0 + jnp.exp(-lw))
        a = 1.0 / (1.0 + jnp.exp(-(vec[2 + d:3 + d] + _bdot(ta, a2_ref[d]))))
        kd = k * (1.0 + (a - 1.0) * vec[5:6])
        k_ref[d] = kd
        b_ref[d] = kk * a
        bonus = bonus + r * kd * vec[6:7]
    bonus_ref[...] = _head_sum(bonus, e_ref) * v


def _rw_prepare(xc, mod4, g, mu, wm, w1, w2, a1, a2, vec, e, layer, grid_rows):
    n_batch, lc, _ = xc.shape
    tiles = lc // ROW_TILE
    per_tile = ROW_TILE // GRID_W
    last = lc // GRID_W - 1

    def full(a):
        return pl.BlockSpec(a.shape, lambda b, t: (0,) * a.ndim)

    tile_spec = pl.BlockSpec((None, ROW_TILE, D), lambda b, t: (b, t, 0))
    dir_spec = pl.BlockSpec((2, None, ROW_TILE, D), lambda b, t: (0, b, t, 0))
    one = jax.ShapeDtypeStruct((n_batch, lc, D), F32)
    two = jax.ShapeDtypeStruct((2, n_batch, lc, D), F32)
    return pl.pallas_call(
        functools.partial(_rw_prep_kernel, grid_rows=grid_rows),
        grid=(n_batch, tiles),
        in_specs=[pl.BlockSpec((None, GRID_W, D), lambda b, t: (b, jnp.maximum(t * per_tile - 1, 0), 0)),
                  tile_spec,
                  pl.BlockSpec((None, GRID_W, D), lambda b, t: (b, jnp.minimum((t + 1) * per_tile, last), 0)),
                  _mod_spec(layer, n_batch, _identity), full(g), full(mu), full(wm), full(w1), full(w2),
                  full(a1), full(a2), full(vec), full(e)],
        out_specs=[tile_spec, tile_spec, tile_spec, dir_spec, dir_spec, dir_spec, tile_spec, tile_spec],
        out_shape=[one, one, one, two, two, two, one, jax.ShapeDtypeStruct((n_batch, lc, D), BF16)],
        compiler_params=_params("parallel", "parallel"),
    )(xc, xc, xc, mod4, g, mu, wm, w1, w2, a1, a2, vec, e)


def _wkv_masks(reverse):
    c, hg = WKV_CHUNK, WKV_GROUP
    t = lax.broadcasted_iota(jnp.int32, (c, hg * c), 0)
    s = lax.broadcasted_iota(jnp.int32, (c, hg * c), 1) % c
    strict = jnp.where((s > t) if reverse else (s < t), 1.0, 0.0)
    incl = jnp.where((s >= t) if reverse else (s <= t), 1.0, 0.0)
    eye = jnp.where(s == t, 1.0, 0.0)
    return strict, jnp.concatenate([incl, incl], axis=1), eye, jnp.concatenate([incl[:, :c]] * 3, axis=1).astype(BF16)


def _wkv_chunks(chains, m4, bd, bd_state):
    c, w, hg = WKV_CHUNK, WKV_LANES, WKV_GROUP
    n = range(len(chains))
    refs, rows, masks, reverse = zip(*[(ch["refs"], (ch["rows"], ch["lanes"]), ch["masks"], ch["reverse"])
                                       for ch in chains])
    strict, incl2, eye, cum_mat = zip(*masks)

    def dot(a, b, dims=None):
        if dims is None:
            return jnp.dot(a, b, preferred_element_type=F32)
        return lax.dot_general(a, b, dims, preferred_element_type=F32)

    def stack(x):
        return jnp.where(jnp.concatenate([m4] * (x.shape[1] // w), axis=1), jnp.concatenate([x] * hg, axis=0), 0)

    def blockdiag(x):
        return jnp.where(bd, jnp.concatenate([x.astype(BF16)] * hg, axis=0), 0)

    ld = [refs[i][5][rows[i]] for i in n]
    cum = [dot(cum_mat[i], jnp.concatenate(_split3(ld[i]), axis=0)) for i in n]
    tot = [cum[i][0:1] if reverse[i] else cum[i][c - 1:c] for i in n]
    e_out = [jnp.exp(-cum[i]) for i in n]
    vb = [refs[i][4][rows[i]].astype(BF16) for i in n]
    rt = [(refs[i][0][rows[i]] * jnp.exp(cum[i])).astype(BF16) for i in n]
    at = [(-refs[i][1][rows[i]] * jnp.exp(cum[i] - ld[i])).astype(BF16) for i in n]
    kt = [(refs[i][3][rows[i]] * e_out[i]).astype(BF16) for i in n]
    bt = [(refs[i][2][rows[i]] * e_out[i]).astype(BF16) for i in n]

    sc = [dot(jnp.concatenate([at[i], rt[i]], axis=0), jnp.concatenate([stack(kt[i]), stack(bt[i])], axis=0), _NT)
          for i in n]
    a_ak = [(sc[i][:c, :w] * strict[i]).astype(BF16) for i in n]
    p = [sc[i][:c, w:] * strict[i] for i in n]
    a_r = [(sc[i][c:] * incl2[i]).astype(BF16) for i in n]
    tinv = [eye[i] + p[i] for i in n]
    for _ in range(c.bit_length() - 2):
        p = [dot(p[i].astype(BF16), blockdiag(p[i])) for i in n]
        tinv = [tinv[i] + dot(p[i].astype(BF16), blockdiag(tinv[i])) for i in n]
    sv = [stack(vb[i]) for i in n]
    akv = [dot(a_ak[i], sv[i]).astype(BF16) for i in n]
    tx = [dot(tinv[i].astype(BF16), stack(jnp.concatenate([at[i], akv[i]], axis=1))) for i in n]

    ht = [ch["ht"][ch["slot"]] for ch in chains]
    wrh = [dot(jnp.concatenate([tx[i][:, :w].astype(BF16), rt[i]], axis=0), ht[i].astype(BF16), _NT) for i in n]
    ub = [(wrh[i][:c] + tx[i][:, w:]).astype(BF16) for i in n]
    y = [wrh[i][c:] + dot(a_r[i], jnp.concatenate([sv[i], stack(ub[i])], axis=0)) for i in n]
    e_end = [jnp.exp(tot[i] - cum[i]) for i in n]
    kb = [jnp.concatenate([(refs[i][3][rows[i]] * e_end[i]).astype(BF16),
                           (refs[i][2][rows[i]] * e_end[i]).astype(BF16)], axis=0) for i in n]
    upd = [dot(jnp.concatenate([vb[i], ub[i]], axis=0), kb[i], _TN) for i in n]
    for i, ch in enumerate(chains):
        ch["y"][rows[i]] = y[i]
        ch["ht"][ch["slot"]] = jnp.where(bd_state, ht[i] * jnp.exp(tot[i]) + upd[i], 0.0)


def _wkv_kernel(rf, kkf, bf, kf, vf, ldf, rb, kkb, bb, kb, vb, ldb, yf_ref, yb_ref, ht_ref):
    @pl.when(pl.program_id(1) == 0)
    def _():
        ht_ref[...] = jnp.zeros_like(ht_ref)

    c, w, hg = WKV_CHUNK, WKV_LANES, WKV_GROUP
    n_chunks = ROW_TILE // c
    lane_head = lax.broadcasted_iota(jnp.int32, (hg * c, w), 1) // RW_HEAD
    row_head = lax.broadcasted_iota(jnp.int32, (hg * c, w), 0) // c
    m4 = lane_head == row_head
    bd = lax.broadcasted_iota(jnp.int32, (hg * c, hg * c), 1) // c == row_head
    bd_state = (lax.broadcasted_iota(jnp.int32, (w, w), 0) // RW_HEAD
                == lax.broadcasted_iota(jnp.int32, (w, w), 1) // RW_HEAD)
    masks_f, masks_b = _wkv_masks(False), _wkv_masks(True)
    groups = D // w

    def body(step, carry):
        rows_f = pl.ds(pl.multiple_of(step * c, c), c)
        rows_b = pl.ds(pl.multiple_of((n_chunks - 1 - step) * c, c), c)
        chains = []
        for g in range(groups):
            lanes = slice(g * w, (g + 1) * w)
            chains.append(dict(refs=(rf, kkf, bf, kf, vf, ldf), rows=rows_f, lanes=lanes, masks=masks_f,
                               reverse=False, ht=ht_ref, slot=g, y=yf_ref))
            chains.append(dict(refs=(rb, kkb, bb, kb, vb, ldb), rows=rows_b, lanes=lanes, masks=masks_b,
                               reverse=True, ht=ht_ref, slot=groups + g, y=yb_ref))
        _wkv_chunks(chains, m4, bd, bd_state)
        return carry

    lax.fori_loop(0, n_chunks, body, 0)


def _wkv(r, kk, b, k, v, ld):
    n_batch, lc, _ = r.shape
    tiles = lc // ROW_TILE
    bt = _bwd_tile(tiles)

    def specs(tile_of, direction):
        one = pl.BlockSpec((None, ROW_TILE, D), lambda bb, s: (bb, tile_of(s), 0))
        two = pl.BlockSpec((None, None, ROW_TILE, D), lambda bb, s: (direction, bb, tile_of(s), 0))
        return one, two

    f1, f2 = specs(_identity, 0)
    b1, b2 = specs(bt, 1)
    return pl.pallas_call(
        _wkv_kernel,
        grid=(n_batch, tiles),
        in_specs=[f1, f1, f2, f2, f1, f2, b1, b1, b2, b2, b1, b2],
        out_specs=[f1, b1],
        out_shape=[jax.ShapeDtypeStruct(r.shape, F32)] * 2,
        scratch_shapes=[pltpu.VMEM((2 * (D // WKV_LANES), WKV_LANES, WKV_LANES), F32)],
        compiler_params=_params("parallel", "arbitrary"),
    )(r, kk, b, k, v, ld, r, kk, b, k, v, ld)


def _rw_out_kernel(yf_ref, yb_ref, bonus_ref, z_ref, x_ref, mod_ref, vec_ref, e_ref, wo_ref, xo_ref):
    y = yf_ref[...] + yb_ref[...]
    inv = 1.0 / RW_HEAD
    yc = y - _head_sum(y, e_ref) * inv
    yn = yc * lax.rsqrt(_head_sum(yc * yc, e_ref) * inv + RW_LNX_EPS)
    yn = yn * vec_ref[0:1] + vec_ref[1:2]
    o = (yn + bonus_ref[...]) * _silu(z_ref[...].astype(F32))
    xo_ref[...] = x_ref[...] + mod_ref[:, 2 * D:] * _bdot(o, wo_ref[...])


def _rw_output(xc, y_f, y_b, bonus, z, mod4, vec, e, w_out, layer):
    n_batch, lc, _ = xc.shape
    tiles = lc // ROW_TILE
    tile_spec = pl.BlockSpec((None, ROW_TILE, D), lambda b, t: (b, t, 0))

    def full(a):
        return pl.BlockSpec(a.shape, lambda b, t: (0,) * a.ndim)

    return pl.pallas_call(
        _rw_out_kernel,
        grid=(n_batch, tiles),
        in_specs=[tile_spec, tile_spec, tile_spec, tile_spec, tile_spec, _mod_spec(layer, n_batch, _identity),
                  full(vec), full(e), full(w_out)],
        out_specs=tile_spec,
        out_shape=jax.ShapeDtypeStruct(xc.shape, F32),
        compiler_params=_params("parallel", "parallel"),
    )(y_f, y_b, bonus, z, xc, mod4, vec, e, w_out)


def _rwkv(xc, mod4, g, mu, w_rkvg, w0, w1, w2, a0, a1, a2, k_k, k_a, r_k, lnx_g, lnx_b, w_out, layer, grid_rows):
    zeros = jnp.zeros((RW_LORA, D), F32)
    w1c = jnp.concatenate([w1[0], w1[1]], axis=1).astype(BF16)
    a1c = jnp.concatenate([a1[0], a1[1]], axis=1).astype(BF16)
    w2p = jnp.stack([jnp.concatenate([w2[0], zeros]), jnp.concatenate([zeros, w2[1]])]).astype(BF16)
    a2p = jnp.stack([jnp.concatenate([a2[0], zeros]), jnp.concatenate([zeros, a2[1]])]).astype(BF16)
    zrow = jnp.zeros((D,), F32)
    vec = jnp.stack([w0[0], w0[1], a0[0], a0[1], k_k, k_a, r_k.reshape(D), zrow])
    lane = jnp.arange(WKV_LANES) // RW_HEAD
    e = (lane[:, None] == lane[None, :]).astype(BF16)
    r, v, kk, k, b, ld, bonus, z = _rw_prepare(xc, mod4, g.reshape(1, D), mu, w_rkvg.astype(BF16), w1c, w2p,
                                               a1c, a2p, vec, e, layer, grid_rows)
    y_f, y_b = _wkv(r, kk, b, k, v, ld)
    vec_o = jnp.stack([lnx_g, lnx_b] + [zrow] * 6)
    return _rw_output(xc, y_f, y_b, bonus, z, mod4, vec_o, e, w_out.astype(BF16), layer)


def kernel(x, c, ctx, c_ctx, ada_w, ada_b, norm_g, final_g, ret_w_in, ret_decay, ret_w_out, gm_w_in, gm_vnorm_g,
           gm_w_s, gm_b_s, gm_w_out, rw_mu, rw_w_rkvg, rw_w0, rw_w1, rw_w2, rw_a0, rw_a1, rw_a2, rw_k_k, rw_k_a,
           rw_r_k, rw_lnx_g, rw_lnx_b, rw_w_out):
    n_batch, seq_len, _ = x.shape
    ctx_len = ctx.shape[1]
    depth = ada_w.shape[0]
    assert ctx_len == ROW_TILE and seq_len % ROW_TILE == 0 and n_batch < 8

    xc = jnp.concatenate([ctx, x], axis=1)
    cvec = jnp.zeros((8, D), F32).at[:n_batch].set(c).at[n_batch].set(c_ctx)
    mod4 = _modulation(cvec, ada_w, ada_b).reshape(depth, 8, 1, 3 * D)
    rope = _rope_table(ctx_len, seq_len)
    k_scale = jnp.concatenate([jnp.ones((RET_HEADS * RET_DK,), F32),
                               jnp.full((RET_HEADS * RET_DK,), RET_DK ** -0.5, F32),
                               jnp.ones((2 * RET_HEADS * RET_DV,), F32)])

    for i in range(depth):
        kind, j = i % N_MIXERS, i // N_MIXERS
        if kind == 0:
            w_in = (ret_w_in[j] * k_scale).astype(BF16)
            proj = _project(_ret_proj_kernel, xc, mod4, norm_g[i], w_in, rope,
                            pl.BlockSpec((ROW_TILE, RET_DK), lambda b, t: (t, 0)), i)
            xc = _retention(xc, proj, mod4, ret_w_out[j].astype(BF16), _ret_tables(ret_decay[j]), i,
                            final_g if i == depth - 1 else None)
        elif kind == 1:
            proj = _project(_gm_proj_kernel, xc, mod4, norm_g[i], gm_w_in[j].astype(BF16),
                            gm_vnorm_g[j].reshape(1, GM_WIDTH), pl.BlockSpec((1, GM_WIDTH), lambda b, t: (0, 0)), i,
                            scratch=[pltpu.VMEM((ROW_TILE, GM_WIDTH), F32)])
            xc = _gmlp(xc, proj, mod4, gm_w_s[j].astype(BF16), gm_b_s[j].T, gm_w_out[j].astype(BF16), i)
        else:
            xc = _rwkv(xc, mod4, norm_g[i], rw_mu[j], rw_w_rkvg[j], rw_w0[j], rw_w1[j], rw_w2[j], rw_a0[j],
                       rw_a1[j], rw_a2[j], rw_k_k[j], rw_k_a[j], rw_r_k[j], rw_lnx_g[j], rw_lnx_b[j],
                       rw_w_out[j], i, seq_len // GRID_W)
    assert (depth - 1) % N_MIXERS == 0, "the final norm is fused into a retention layer"
    return xc
```

```python
import functools
import math

import jax
import jax.numpy as jnp
from jax import lax
from jax.experimental import pallas as pl
from jax.experimental.pallas import tpu as pltpu

F32 = jnp.float32
BF16 = jnp.bfloat16

D = 1024
GRID_W = 64
NORM_EPS = 1e-6
N_MIXERS = 3
ROW_TILE = 256
RET_HEADS = 4
RET_DK = 256
RET_DV = 512
ROPE_BASE = 10000.0
GM_WIDTH = 2 * D
GM_GROUPS = 8
GM_CHUNK = 128
RW_HEAD = 64
RW_LORA = 64
RW_LNX_EPS = 64e-5
WKV_CHUNK = 64
WKV_LANES = 256
WKV_GROUP = WKV_LANES // RW_HEAD
VMEM_LIMIT = 56 * 1024 * 1024

_NT = (((1,), (1,)), ((), ()))
_TN = (((0,), (0,)), ((), ()))


def _params(*sem):
    return pltpu.CompilerParams(dimension_semantics=sem, vmem_limit_bytes=VMEM_LIMIT)


def _silu(x):
    return x / (1.0 + jnp.exp(-x))


def _bdot(a, b, dims=None):
    a = a.astype(BF16)
    b = b.astype(BF16)
    if dims is None:
        return jnp.dot(a, b, preferred_element_type=F32)
    return lax.dot_general(a, b, dims, preferred_element_type=F32)


def _split3(x):
    hi = x.astype(BF16)
    r1 = x - hi.astype(F32)
    mid = r1.astype(BF16)
    lo = (r1 - mid.astype(F32)).astype(BF16)
    return hi, mid, lo


def _norm_mod(x, g, mod):
    ms = jnp.mean(x * x, axis=-1, keepdims=True)
    y = x * lax.rsqrt(ms + NORM_EPS) * g
    return y * (1.0 + mod[:, D:2 * D]) + mod[:, :D]


def _mod_spec(layer, n_batch, tile_of_step):
    def index(b, s):
        t = tile_of_step(s)
        return (layer, jnp.where(t == 0, n_batch, b), 0, 0)
    return pl.BlockSpec((None, None, 1, 3 * D), index)


def _identity(s):
    return s


def _mod_kernel(c_ref, w_ref, b_ref, o_ref):
    s = _silu(c_ref[...])
    o_ref[...] = jnp.dot(s, w_ref[...], preferred_element_type=F32,
                         precision=lax.Precision.HIGHEST) + b_ref[...]


def _modulation(cvec, ada_w, ada_b):
    depth = ada_w.shape[0]
    return pl.pallas_call(
        _mod_kernel,
        grid=(depth, 3),
        in_specs=[pl.BlockSpec((8, D), lambda i, j: (0, 0)),
                  pl.BlockSpec((None, D, D), lambda i, j: (i, 0, j)),
                  pl.BlockSpec((None, 1, D), lambda i, j: (i, 0, j))],
        out_specs=pl.BlockSpec((None, 8, D), lambda i, j: (i, 0, j)),
        out_shape=jax.ShapeDtypeStruct((depth, 8, 3 * D), F32),
        compiler_params=_params("arbitrary", "arbitrary"),
    )(cvec, ada_w, ada_b.reshape(depth, 1, 3 * D))


PROJ_COLS = 512


def _ret_proj_kernel(x_ref, mod_ref, g_ref, w_ref, cs_ref, o_ref):
    h = _norm_mod(x_ref[...], g_ref[...], mod_ref[...]).astype(BF16)
    half = RET_DK // 2
    qk, zc = 2 * RET_HEADS * RET_DK, 2 * RET_HEADS * RET_DK + RET_HEADS * RET_DV
    cos = cs_ref[:, :half]
    sin = cs_ref[:, half:]
    for c0 in range(0, qk, RET_DK):
        acc = jnp.dot(h, w_ref[:, c0:c0 + RET_DK], preferred_element_type=F32)
        t1, t2 = acc[:, :half], acc[:, half:]
        o_ref[:, c0:c0 + half] = (t1 * cos - t2 * sin).astype(BF16)
        o_ref[:, c0 + half:c0 + RET_DK] = (t1 * sin + t2 * cos).astype(BF16)
    for c0 in range(qk, o_ref.shape[-1], PROJ_COLS):
        acc = jnp.dot(h, w_ref[:, c0:c0 + PROJ_COLS], preferred_element_type=F32)
        o_ref[:, c0:c0 + PROJ_COLS] = (_silu(acc) if c0 >= zc else acc).astype(BF16)


def _ret_project(xc, mod4, g, w, rope, layer):
    n_batch, lc, _ = xc.shape
    n_out = w.shape[1]
    tiles = lc // ROW_TILE
    return pl.pallas_call(
        _ret_proj_kernel,
        grid=(n_batch, tiles),
        in_specs=[pl.BlockSpec((None, ROW_TILE, D), lambda b, t: (b, t, 0)),
                  _mod_spec(layer, n_batch, _identity),
                  pl.BlockSpec((1, D), lambda b, t: (0, 0)),
                  pl.BlockSpec((D, n_out), lambda b, t: (0, 0)),
                  pl.BlockSpec((ROW_TILE, RET_DK), lambda b, t: (t, 0))],
        out_specs=pl.BlockSpec((None, ROW_TILE, n_out), lambda b, t: (b, t, 0)),
        out_shape=jax.ShapeDtypeStruct((n_batch, lc, n_out), BF16),
        compiler_params=_params("parallel", "parallel"),
    )(xc, mod4, g.reshape(1, D), w, rope)


def _rope_table(ctx_len, seq_len):
    n_freq = RET_DK // 4
    freqs = ROPE_BASE ** (-jnp.arange(n_freq, dtype=F32) / n_freq)
    t = jnp.arange(seq_len, dtype=jnp.int32)
    row, col = (t // GRID_W).astype(F32), (t % GRID_W).astype(F32)
    ang = jnp.concatenate([row[:, None] * freqs, col[:, None] * freqs], axis=-1)
    ang = jnp.concatenate([jnp.zeros((ctx_len, RET_DK // 2), F32), ang], axis=0)
    return jnp.concatenate([jnp.cos(ang), jnp.sin(ang)], axis=-1)


RET_ROW_LANES = 128


def _ret_table_kernel(dl_ref, mask_ref, row_ref, cd_ref):
    c = ROW_TILE
    i = lax.broadcasted_iota(jnp.int32, (c, c), 0).astype(F32)
    j = lax.broadcasted_iota(jnp.int32, (c, c), 1).astype(F32)
    ir = i[:, :RET_ROW_LANES]

    def log_sigmoid(x):
        return jnp.minimum(x, 0.0) - jnp.log(1.0 + jnp.exp(-jnp.abs(x)))

    for h in range(RET_HEADS):
        lg_f = log_sigmoid(jnp.full((c, c), dl_ref[0, h], F32))
        lg_b = log_sigmoid(jnp.full((c, c), dl_ref[1, h], F32))
        fwd = jnp.where(i >= j, jnp.exp(jnp.maximum(i - j, 0.0) * lg_f), 0.0)
        bwd = jnp.where(j >= i, jnp.exp(jnp.maximum(j - i, 0.0) * lg_b), 0.0)
        mask_ref[h] = fwd + bwd
        lf = log_sigmoid(jnp.full((c, RET_ROW_LANES), dl_ref[0, h], F32))
        lb = log_sigmoid(jnp.full((c, RET_ROW_LANES), dl_ref[1, h], F32))
        row_ref[h, 0] = jnp.exp((ir + 1.0) * lf)
        row_ref[h, 1] = jnp.exp((c - 1.0 - ir) * lf)
        row_ref[h, 2] = jnp.exp((c - ir) * lb)
        row_ref[h, 3] = jnp.exp(ir * lb)
        cd_ref[h, 0] = jnp.exp(c * log_sigmoid(jnp.full((8, RET_DV), dl_ref[0, h], F32)))
        cd_ref[h, 1] = jnp.exp(c * log_sigmoid(jnp.full((8, RET_DV), dl_ref[1, h], F32)))


def _ret_tables(decay_logit):
    c = ROW_TILE
    return pl.pallas_call(
        _ret_table_kernel,
        in_specs=[pl.BlockSpec(memory_space=pltpu.SMEM)],
        out_shape=(jax.ShapeDtypeStruct((RET_HEADS, c, c), F32),
                   jax.ShapeDtypeStruct((RET_HEADS, 4, c, RET_ROW_LANES), F32),
                   jax.ShapeDtypeStruct((RET_HEADS, 2, 8, RET_DV), F32)),
    )(decay_logit)


def _lanes(t, width):
    return jnp.concatenate([t] * (width // RET_ROW_LANES), axis=1)


def _ret_heads(q_ref, k_ref, v_ref):
    return [(q_ref[:, h * RET_DK:(h + 1) * RET_DK], k_ref[:, h * RET_DK:(h + 1) * RET_DK],
             v_ref[:, h * RET_DV:(h + 1) * RET_DV]) for h in range(RET_HEADS)]


def _ret_fwd_kernel(q_ref, k_ref, v_ref, mask_ref, row_ref, cd_ref, o_ref, s_ref):
    @pl.when(pl.program_id(1) == 0)
    def _():
        s_ref[...] = jnp.zeros_like(s_ref)

    qkv = _ret_heads(q_ref, k_ref, v_ref)
    sc = {}

    def scores(h):
        q, k, _ = qkv[h]
        sc[h] = lax.dot_general(q, k, _NT, preferred_element_type=F32)

    def finish(h):
        q, k, v = qkv[h]
        s = s_ref[h]
        qs = _bdot(q, s)
        kd = k.astype(F32) * _lanes(row_ref[h, 1], RET_DK)
        s_ref[h] = s * cd_ref[h, 0, 0:1, :] + _bdot(kd, v, _TN)
        o = _bdot(sc[h] * mask_ref[h], v)
        o_ref[:, h * RET_DV:(h + 1) * RET_DV] = (o + qs * _lanes(row_ref[h, 0], RET_DV)).astype(BF16)

    scores(0)
    for h in range(RET_HEADS):
        if h + 1 < RET_HEADS:
            scores(h + 1)
        finish(h)


def _ret_bwd_kernel(q_ref, k_ref, v_ref, z_ref, op_ref, x_ref, mod_ref, wo_ref, row_ref, cd_ref, fg_ref,
                    xo_ref, s_ref, *, final):
    @pl.when(pl.program_id(1) == 0)
    def _():
        s_ref[...] = jnp.zeros_like(s_ref)

    qkv = _ret_heads(q_ref, k_ref, v_ref)
    gz = {}

    def inter(h):
        q, k, v = qkv[h]
        vv = slice(h * RET_DV, (h + 1) * RET_DV)
        s = s_ref[h]
        o = op_ref[:, vv].astype(F32) + _bdot(q, s) * _lanes(row_ref[h, 2], RET_DV)
        kd = k.astype(F32) * _lanes(row_ref[h, 3], RET_DK)
        s_ref[h] = s * cd_ref[h, 1, 0:1, :] + _bdot(kd, v, _TN)
        o = o * lax.rsqrt(jnp.mean(o * o, axis=-1, keepdims=True) + NORM_EPS)
        gz[h] = (o * z_ref[:, vv].astype(F32)).astype(BF16)

    acc = jnp.zeros((ROW_TILE, D), F32)
    inter(0)
    for h in range(RET_HEADS):
        if h + 1 < RET_HEADS:
            inter(h + 1)
        acc = acc + jnp.dot(gz[h], wo_ref[h * RET_DV:(h + 1) * RET_DV, :], preferred_element_type=F32)
    xn = x_ref[...] + mod_ref[:, 2 * D:] * acc
    if final:
        xn = xn * lax.rsqrt(jnp.mean(xn * xn, axis=-1, keepdims=True) + NORM_EPS) * fg_ref[...]
    xo_ref[...] = xn


def _bwd_tile(tiles):
    return lambda s: jnp.where(s == 0, 0, tiles - s)


def _retention(xc, proj, mod4, w_out, tables, layer, final_g):
    n_batch, lc, _ = xc.shape
    tiles = lc // ROW_TILE
    width = RET_HEADS * RET_DV
    state = pltpu.VMEM((RET_HEADS, RET_DK, RET_DV), F32)
    mask, row, cd = tables
    mask_spec = pl.BlockSpec(mask.shape, lambda b, s: (0, 0, 0))
    row_spec = pl.BlockSpec(row.shape, lambda b, s: (0, 0, 0, 0))
    cd_spec = pl.BlockSpec(cd.shape, lambda b, s: (0, 0, 0, 0))

    def col_spec(width_, col, tile_of_step):
        return pl.BlockSpec((None, ROW_TILE, width_), lambda b, s: (b, tile_of_step(s), col))

    o_part = pl.pallas_call(
        _ret_fwd_kernel,
        grid=(n_batch, tiles),
        in_specs=[col_spec(RET_HEADS * RET_DK, 0, _identity), col_spec(RET_HEADS * RET_DK, 1, _identity),
                  col_spec(width, 1, _identity), mask_spec, row_spec, cd_spec],
        out_specs=col_spec(width, 0, _identity),
        out_shape=jax.ShapeDtypeStruct((n_batch, lc, width), BF16),
        scratch_shapes=[state],
        compiler_params=_params("parallel", "arbitrary"),
    )(proj, proj, proj, mask, row, cd)

    bt = _bwd_tile(tiles)
    final = final_g is not None
    if final:
        ctx_tiles = 1
        out_spec = col_spec(D, 0, lambda s: bt(jnp.maximum(s, 1)) - ctx_tiles)
        out_shape = jax.ShapeDtypeStruct((n_batch, lc - ctx_tiles * ROW_TILE, D), F32)
        fg = final_g.reshape(1, D)
    else:
        out_spec = col_spec(D, 0, bt)
        out_shape = jax.ShapeDtypeStruct(xc.shape, F32)
        fg = jnp.ones((1, D), F32)
    return pl.pallas_call(
        functools.partial(_ret_bwd_kernel, final=final),
        grid=(n_batch, tiles),
        in_specs=[col_spec(RET_HEADS * RET_DK, 0, bt), col_spec(RET_HEADS * RET_DK, 1, bt),
                  col_spec(width, 1, bt), col_spec(width, 2, bt), col_spec(width, 0, bt),
                  col_spec(D, 0, bt), _mod_spec(layer, n_batch, bt),
                  pl.BlockSpec((width, D), lambda b, s: (0, 0)), row_spec, cd_spec,
                  pl.BlockSpec((1, D), lambda b, s: (0, 0))],
        out_specs=out_spec,
        out_shape=out_shape,
        scratch_shapes=[state],
        compiler_params=_params("parallel", "arbitrary"),
    )(proj, proj, proj, proj, o_part, xc, mod4, w_out, row, cd, fg)


def _gm_kernel(x_ref, mod_ref, g_ref, w_ref, vg_ref, ws_ref, bs_ref, wo_ref, xo_ref, v_ref):
    mod = mod_ref[...]
    h = _norm_mod(x_ref[...], g_ref[...], mod).astype(BF16)
    for c0 in range(0, GM_WIDTH, PROJ_COLS):
        v_ref[:, c0:c0 + PROJ_COLS] = jnp.dot(h, w_ref[:, GM_WIDTH + c0:GM_WIDTH + c0 + PROJ_COLS],
                                              preferred_element_type=F32)
    v = v_ref[...]
    v = v - jnp.mean(v, axis=-1, keepdims=True)
    vn = (v * lax.rsqrt(jnp.mean(v * v, axis=-1, keepdims=True) + NORM_EPS) * vg_ref[...]).astype(BF16)

    c = GM_CHUNK
    gw = GM_WIDTH // GM_GROUPS
    gated = {}

    def group(g):
        cols = slice(g * gw, (g + 1) * gw)
        u = jnp.dot(h, w_ref[:, cols], preferred_element_type=F32)
        z = jnp.dot(h, w_ref[:, 2 * GM_WIDTH + g * gw:2 * GM_WIDTH + (g + 1) * gw], preferred_element_type=F32)
        mixed = jnp.concatenate([jnp.dot(ws_ref[g], vn[ci * c:(ci + 1) * c, cols], preferred_element_type=F32)
                                 for ci in range(ROW_TILE // c)], axis=0)
        mixed = mixed + jnp.concatenate([bs_ref[:, g:g + 1]] * (ROW_TILE // c), axis=0)
        gated[g] = (u * mixed * _silu(z)).astype(BF16)

    acc = jnp.zeros((ROW_TILE, D), F32)
    group(0)
    for g in range(GM_GROUPS):
        if g + 1 < GM_GROUPS:
            group(g + 1)
        acc = acc + jnp.dot(gated[g], wo_ref[g * gw:(g + 1) * gw, :], preferred_element_type=F32)
    xo_ref[...] = x_ref[...] + mod[:, 2 * D:] * acc


def _gmlp(xc, mod4, g, w_in, vnorm_g, w_s, b_s, w_out, layer):
    n_batch, lc, _ = xc.shape
    tiles = lc // ROW_TILE
    tile_spec = pl.BlockSpec((None, ROW_TILE, D), lambda b, t: (b, t, 0))

    def full(a):
        return pl.BlockSpec(a.shape, lambda b, t: (0,) * a.ndim)

    return pl.pallas_call(
        _gm_kernel,
        grid=(n_batch, tiles),
        in_specs=[tile_spec, _mod_spec(layer, n_batch, _identity), full(g), full(w_in), full(vnorm_g), full(w_s),
                  full(b_s), full(w_out)],
        out_specs=tile_spec,
        out_shape=jax.ShapeDtypeStruct(xc.shape, F32),
        scratch_shapes=[pltpu.VMEM((ROW_TILE, GM_WIDTH), F32)],
        compiler_params=_params("parallel", "parallel"),
    )(xc, mod4, g, w_in, vnorm_g, w_s, b_s, w_out)


def _head_sum(x, e_ref):
    outs = []
    e = e_ref[...]
    for g in range(x.shape[-1] // WKV_LANES):
        xs = x[:, g * WKV_LANES:(g + 1) * WKV_LANES]
        hi = xs.astype(BF16)
        lo = (xs - hi.astype(F32)).astype(BF16)
        outs.append(jnp.dot(hi, e, preferred_element_type=F32) + jnp.dot(lo, e, preferred_element_type=F32))
    return jnp.concatenate(outs, axis=1)


def _rw_prep_kernel(xp_ref, x_ref, xn_ref, mod_ref, g_ref, mu_ref, wm_ref, w1_ref, w2_ref, a1_ref, a2_ref,
                    vec_ref, e_ref, r_ref, v_ref, kk_ref, k_ref, b_ref, ld_ref, bonus_ref, z_ref,
                    *, grid_rows):
    t = pl.program_id(1)
    is_ctx = t == 0
    g, mod = g_ref[...], mod_ref[...]
    h = _norm_mod(x_ref[...], g, mod)
    h_up = _norm_mod(xp_ref[...], g, mod)
    h_dn = _norm_mod(xn_ref[...], g, mod)
    n = ROW_TILE
    prev = pltpu.roll(h, 1, 0)
    nxt = pltpu.roll(h, n - 1, 0)
    ext = jnp.concatenate([h_up, h, h_dn], axis=0)
    up, down = ext[:n], ext[2 * GRID_W:]

    lane_q = lax.broadcasted_iota(jnp.int32, (n, D), 1) // (D // 4)
    i = lax.broadcasted_iota(jnp.int32, (n, D), 0)
    grow = (t - 1) * (n // GRID_W) + i // GRID_W
    pos = jnp.where(is_ctx, i, i % GRID_W)
    last_pos = jnp.where(is_ctx, n - 1, GRID_W - 1)
    prev_lanes = jnp.where(is_ctx, 2, 1)
    next_lanes = jnp.where(is_ctx, 4, 2)
    prev = jnp.where(pos > 0, prev, 0.0)
    nxt = jnp.where(pos < last_pos, nxt, 0.0)
    up = jnp.where(grow > 0, up, 0.0)
    down = jnp.where(grow < grid_rows - 1, down, 0.0)
    shifted = jnp.where(lane_q < prev_lanes, prev,
                        jnp.where(lane_q < next_lanes, nxt, jnp.where(lane_q == 2, up, down)))
    xx = shifted - h

    hb, xb = h.astype(BF16), xx.astype(BF16)

    def mix(p):
        return hb + xb * mu_ref[p:p + 1, :].astype(BF16)

    vec = vec_ref[...]
    r = jnp.dot(mix(0), wm_ref[0], preferred_element_type=F32)
    k = jnp.dot(mix(2), wm_ref[1], preferred_element_type=F32)
    v = jnp.dot(mix(3), wm_ref[2], preferred_element_type=F32)
    z_ref[...] = jnp.dot(mix(5), wm_ref[3], preferred_element_type=F32).astype(BF16)
    tw = jnp.tanh(jnp.dot(mix(1), w1_ref[...], preferred_element_type=F32))
    ta = jnp.dot(mix(4), a1_ref[...], preferred_element_type=F32)

    kk = k * vec[4:5]
    kk = kk / jnp.maximum(jnp.sqrt(_head_sum(kk * kk, e_ref)), 1e-12)
    r_ref[...] = r
    v_ref[...] = v
    kk_ref[...] = kk
    bonus = jnp.zeros_like(r)
    for d in range(2):
        lw = vec[d:d + 1] + _bdot(tw, w2_ref[d])
        ld_ref[d] = -math.exp(-0.5) / (1.0 + jnp.exp(-lw))
        a = 1.0 / (1.0 + jnp.exp(-(vec[2 + d:3 + d] + _bdot(ta, a2_ref[d]))))
        kd = k * (1.0 + (a - 1.0) * vec[5:6])
        k_ref[d] = kd
        b_ref[d] = kk * a
        bonus = bonus + r * kd * vec[6:7]
    bonus_ref[...] = (_head_sum(bonus, e_ref) * v).astype(BF16)


def _rw_prepare(xc, mod4, g, mu, wm, w1, w2, a1, a2, vec, e, layer, grid_rows):
    n_batch, lc, _ = xc.shape
    tiles = lc // ROW_TILE
    per_tile = ROW_TILE // GRID_W
    last = lc // GRID_W - 1

    def full(a):
        return pl.BlockSpec(a.shape, lambda b, t: (0,) * a.ndim)

    tile_spec = pl.BlockSpec((None, ROW_TILE, D), lambda b, t: (b, t, 0))
    dir_spec = pl.BlockSpec((2, None, ROW_TILE, D), lambda b, t: (0, b, t, 0))
    one = jax.ShapeDtypeStruct((n_batch, lc, D), F32)
    two = jax.ShapeDtypeStruct((2, n_batch, lc, D), F32)
    half = jax.ShapeDtypeStruct((n_batch, lc, D), BF16)
    return pl.pallas_call(
        functools.partial(_rw_prep_kernel, grid_rows=grid_rows),
        grid=(n_batch, tiles),
        in_specs=[pl.BlockSpec((None, GRID_W, D), lambda b, t: (b, jnp.maximum(t * per_tile - 1, 0), 0)),
                  tile_spec,
                  pl.BlockSpec((None, GRID_W, D), lambda b, t: (b, jnp.minimum((t + 1) * per_tile, last), 0)),
                  _mod_spec(layer, n_batch, _identity), full(g), full(mu), full(wm), full(w1), full(w2),
                  full(a1), full(a2), full(vec), full(e)],
        out_specs=[tile_spec, tile_spec, tile_spec, dir_spec, dir_spec, dir_spec, tile_spec, tile_spec],
        out_shape=[one, one, one, two, two, two, half, half],
        compiler_params=_params("parallel", "parallel"),
    )(xc, xc, xc, mod4, g, mu, wm, w1, w2, a1, a2, vec, e)


def _wkv_masks(reverse):
    c, hg = WKV_CHUNK, WKV_GROUP
    t = lax.broadcasted_iota(jnp.int32, (c, hg * c), 0)
    s = lax.broadcasted_iota(jnp.int32, (c, hg * c), 1) % c
    strict = jnp.where((s > t) if reverse else (s < t), 1.0, 0.0)
    incl = jnp.where((s >= t) if reverse else (s <= t), 1.0, 0.0)
    eye = jnp.where(s == t, 1.0, 0.0)
    return strict, jnp.concatenate([incl, incl], axis=1), eye, jnp.concatenate([incl[:, :c]] * 3, axis=1).astype(BF16)


def _wkv_chunks(chains, m4, bd, bd_state):
    c, w, hg = WKV_CHUNK, WKV_LANES, WKV_GROUP
    n = range(len(chains))
    refs, rows, masks, reverse = zip(*[(ch["refs"], (ch["rows"], ch["lanes"]), ch["masks"], ch["reverse"])
                                       for ch in chains])
    strict, incl2, eye, cum_mat = zip(*masks)

    def dot(a, b, dims=None):
        if dims is None:
            return jnp.dot(a, b, preferred_element_type=F32)
        return lax.dot_general(a, b, dims, preferred_element_type=F32)

    def stack(x):
        return jnp.where(jnp.concatenate([m4] * (x.shape[1] // w), axis=1), jnp.concatenate([x] * hg, axis=0), 0)

    def blockdiag(x):
        return jnp.where(bd, jnp.concatenate([x.astype(BF16)] * hg, axis=0), 0)

    ld = [refs[i][5][rows[i]] for i in n]
    cum = [dot(cum_mat[i], jnp.concatenate(_split3(ld[i]), axis=0)) for i in n]
    tot = [cum[i][0:1] if reverse[i] else cum[i][c - 1:c] for i in n]
    e_out = [jnp.exp(-cum[i]) for i in n]
    vb = [refs[i][4][rows[i]].astype(BF16) for i in n]
    rt = [(refs[i][0][rows[i]] * jnp.exp(cum[i])).astype(BF16) for i in n]
    at = [(-refs[i][1][rows[i]] * jnp.exp(cum[i] - ld[i])).astype(BF16) for i in n]
    kt = [(refs[i][3][rows[i]] * e_out[i]).astype(BF16) for i in n]
    bt = [(refs[i][2][rows[i]] * e_out[i]).astype(BF16) for i in n]

    sc = [dot(jnp.concatenate([at[i], rt[i]], axis=0), jnp.concatenate([stack(kt[i]), stack(bt[i])], axis=0), _NT)
          for i in n]
    a_ak = [(sc[i][:c, :w] * strict[i]).astype(BF16) for i in n]
    p = [sc[i][:c, w:] * strict[i] for i in n]
    a_r = [(sc[i][c:] * incl2[i]).astype(BF16) for i in n]
    tinv = [eye[i] + p[i] for i in n]
    for _ in range(c.bit_length() - 2):
        p = [dot(p[i].astype(BF16), blockdiag(p[i])) for i in n]
        tinv = [tinv[i] + dot(p[i].astype(BF16), blockdiag(tinv[i])) for i in n]
    sv = [stack(vb[i]) for i in n]
    akv = [dot(a_ak[i], sv[i]).astype(BF16) for i in n]
    tx = [dot(tinv[i].astype(BF16), stack(jnp.concatenate([at[i], akv[i]], axis=1))) for i in n]

    ht = [ch["ht"][ch["slot"]] for ch in chains]
    wrh = [dot(jnp.concatenate([tx[i][:, :w].astype(BF16), rt[i]], axis=0), ht[i].astype(BF16), _NT) for i in n]
    ub = [(wrh[i][:c] + tx[i][:, w:]).astype(BF16) for i in n]
    y = [wrh[i][c:] + dot(a_r[i], jnp.concatenate([sv[i], stack(ub[i])], axis=0)) for i in n]
    e_end = [jnp.exp(tot[i] - cum[i]) for i in n]
    kb = [jnp.concatenate([(refs[i][3][rows[i]] * e_end[i]).astype(BF16),
                           (refs[i][2][rows[i]] * e_end[i]).astype(BF16)], axis=0) for i in n]
    upd = [dot(jnp.concatenate([vb[i], ub[i]], axis=0), kb[i], _TN) for i in n]
    for i, ch in enumerate(chains):
        ch["y"][rows[i]] = y[i].astype(BF16)
        ch["ht"][ch["slot"]] = jnp.where(bd_state, ht[i] * jnp.exp(tot[i]) + upd[i], 0.0)


def _wkv_kernel(rf, kkf, bf, kf, vf, ldf, rb, kkb, bb, kb, vb, ldb, yf_ref, yb_ref, ht_ref):
    @pl.when(pl.program_id(1) == 0)
    def _():
        ht_ref[...] = jnp.zeros_like(ht_ref)

    c, w, hg = WKV_CHUNK, WKV_LANES, WKV_GROUP
    n_chunks = ROW_TILE // c
    lane_head = lax.broadcasted_iota(jnp.int32, (hg * c, w), 1) // RW_HEAD
    row_head = lax.broadcasted_iota(jnp.int32, (hg * c, w), 0) // c
    m4 = lane_head == row_head
    bd = lax.broadcasted_iota(jnp.int32, (hg * c, hg * c), 1) // c == row_head
    bd_state = (lax.broadcasted_iota(jnp.int32, (w, w), 0) // RW_HEAD
                == lax.broadcasted_iota(jnp.int32, (w, w), 1) // RW_HEAD)
    masks_f, masks_b = _wkv_masks(False), _wkv_masks(True)
    groups = D // w

    def body(step, carry):
        rows_f = pl.ds(pl.multiple_of(step * c, c), c)
        rows_b = pl.ds(pl.multiple_of((n_chunks - 1 - step) * c, c), c)
        chains = []
        for g in range(groups):
            lanes = slice(g * w, (g + 1) * w)
            chains.append(dict(refs=(rf, kkf, bf, kf, vf, ldf), rows=rows_f, lanes=lanes, masks=masks_f,
                               reverse=False, ht=ht_ref, slot=g, y=yf_ref))
            chains.append(dict(refs=(rb, kkb, bb, kb, vb, ldb), rows=rows_b, lanes=lanes, masks=masks_b,
                               reverse=True, ht=ht_ref, slot=groups + g, y=yb_ref))
        _wkv_chunks(chains, m4, bd, bd_state)
        return carry

    lax.fori_loop(0, n_chunks, body, 0)


def _wkv(r, kk, b, k, v, ld):
    n_batch, lc, _ = r.shape
    tiles = lc // ROW_TILE
    bt = _bwd_tile(tiles)

    def specs(tile_of, direction):
        one = pl.BlockSpec((None, ROW_TILE, D), lambda bb, s: (bb, tile_of(s), 0))
        two = pl.BlockSpec((None, None, ROW_TILE, D), lambda bb, s: (direction, bb, tile_of(s), 0))
        return one, two

    f1, f2 = specs(_identity, 0)
    b1, b2 = specs(bt, 1)
    return pl.pallas_call(
        _wkv_kernel,
        grid=(n_batch, tiles),
        in_specs=[f1, f1, f2, f2, f1, f2, b1, b1, b2, b2, b1, b2],
        out_specs=[f1, b1],
        out_shape=[jax.ShapeDtypeStruct(r.shape, BF16)] * 2,
        scratch_shapes=[pltpu.VMEM((2 * (D // WKV_LANES), WKV_LANES, WKV_LANES), F32)],
        compiler_params=_params("parallel", "arbitrary"),
    )(r, kk, b, k, v, ld, r, kk, b, k, v, ld)


def _rw_out_kernel(yf_ref, yb_ref, bonus_ref, z_ref, x_ref, mod_ref, vec_ref, e_ref, wo_ref, xo_ref):
    y = yf_ref[...].astype(F32) + yb_ref[...].astype(F32)
    inv = 1.0 / RW_HEAD
    yc = y - _head_sum(y, e_ref) * inv
    yn = yc * lax.rsqrt(_head_sum(yc * yc, e_ref) * inv + RW_LNX_EPS)
    yn = yn * vec_ref[0:1] + vec_ref[1:2]
    o = (yn + bonus_ref[...].astype(F32)) * _silu(z_ref[...].astype(F32))
    xo_ref[...] = x_ref[...] + mod_ref[:, 2 * D:] * _bdot(o, wo_ref[...])


def _rw_output(xc, y_f, y_b, bonus, z, mod4, vec, e, w_out, layer):
    n_batch, lc, _ = xc.shape
    tiles = lc // ROW_TILE
    tile_spec = pl.BlockSpec((None, ROW_TILE, D), lambda b, t: (b, t, 0))

    def full(a):
        return pl.BlockSpec(a.shape, lambda b, t: (0,) * a.ndim)

    return pl.pallas_call(
        _rw_out_kernel,
        grid=(n_batch, tiles),
        in_specs=[tile_spec, tile_spec, tile_spec, tile_spec, tile_spec, _mod_spec(layer, n_batch, _identity),
                  full(vec), full(e), full(w_out)],
        out_specs=tile_spec,
        out_shape=jax.ShapeDtypeStruct(xc.shape, F32),
        compiler_params=_params("parallel", "parallel"),
    )(y_f, y_b, bonus, z, xc, mod4, vec, e, w_out)


def _rwkv(xc, mod4, g, mu, w_rkvg, w0, w1, w2, a0, a1, a2, k_k, k_a, r_k, lnx_g, lnx_b, w_out, layer, grid_rows):
    zeros = jnp.zeros((RW_LORA, D), F32)
    w1c = jnp.concatenate([w1[0], w1[1]], axis=1).astype(BF16)
    a1c = jnp.concatenate([a1[0], a1[1]], axis=1).astype(BF16)
    w2p = jnp.stack([jnp.concatenate([w2[0], zeros]), jnp.concatenate([zeros, w2[1]])]).astype(BF16)
    a2p = jnp.stack([jnp.concatenate([a2[0], zeros]), jnp.concatenate([zeros, a2[1]])]).astype(BF16)
    zrow = jnp.zeros((D,), F32)
    vec = jnp.stack([w0[0], w0[1], a0[0], a0[1], k_k, k_a, r_k.reshape(D), zrow])
    lane = jnp.arange(WKV_LANES) // RW_HEAD
    e = (lane[:, None] == lane[None, :]).astype(BF16)
    r, v, kk, k, b, ld, bonus, z = _rw_prepare(xc, mod4, g.reshape(1, D), mu, w_rkvg.astype(BF16), w1c, w2p,
                                               a1c, a2p, vec, e, layer, grid_rows)
    y_f, y_b = _wkv(r, kk, b, k, v, ld)
    vec_o = jnp.stack([lnx_g, lnx_b] + [zrow] * 6)
    return _rw_output(xc, y_f, y_b, bonus, z, mod4, vec_o, e, w_out.astype(BF16), layer)


def kernel(x, c, ctx, c_ctx, ada_w, ada_b, norm_g, final_g, ret_w_in, ret_decay, ret_w_out, gm_w_in, gm_vnorm_g,
           gm_w_s, gm_b_s, gm_w_out, rw_mu, rw_w_rkvg, rw_w0, rw_w1, rw_w2, rw_a0, rw_a1, rw_a2, rw_k_k, rw_k_a,
           rw_r_k, rw_lnx_g, rw_lnx_b, rw_w_out):
    n_batch, seq_len, _ = x.shape
    ctx_len = ctx.shape[1]
    depth = ada_w.shape[0]
    assert ctx_len == ROW_TILE and seq_len % ROW_TILE == 0 and n_batch < 8

    xc = jnp.concatenate([ctx, x], axis=1)
    cvec = jnp.zeros((8, D), F32).at[:n_batch].set(c).at[n_batch].set(c_ctx)
    mod4 = _modulation(cvec, ada_w, ada_b).reshape(depth, 8, 1, 3 * D)
    rope = _rope_table(ctx_len, seq_len)
    k_scale = jnp.concatenate([jnp.ones((RET_HEADS * RET_DK,), F32),
                               jnp.full((RET_HEADS * RET_DK,), RET_DK ** -0.5, F32),
                               jnp.ones((2 * RET_HEADS * RET_DV,), F32)])

    for i in range(depth):
        kind, j = i % N_MIXERS, i // N_MIXERS
        if kind == 0:
            w_in = (ret_w_in[j] * k_scale).astype(BF16)
            proj = _ret_project(xc, mod4, norm_g[i], w_in, rope, i)
            xc = _retention(xc, proj, mod4, ret_w_out[j].astype(BF16), _ret_tables(ret_decay[j]), i,
                            final_g if i == depth - 1 else None)
        elif kind == 1:
            xc = _gmlp(xc, mod4, norm_g[i].reshape(1, D), gm_w_in[j].astype(BF16), gm_vnorm_g[j].reshape(1, GM_WIDTH),
                       gm_w_s[j].astype(BF16), gm_b_s[j].T, gm_w_out[j].astype(BF16), i)
        else:
            xc = _rwkv(xc, mod4, norm_g[i], rw_mu[j], rw_w_rkvg[j], rw_w0[j], rw_w1[j], rw_w2[j], rw_a0[j],
                       rw_a1[j], rw_a2[j], rw_k_k[j], rw_k_a[j], rw_r_k[j], rw_lnx_g[j], rw_lnx_b[j],
                       rw_w_out[j], i, seq_len // GRID_W)
    assert (depth - 1) % N_MIXERS == 0, "the final norm is fused into a retention layer"
    return xc
```

```python
import functools
import math

import jax
import jax.numpy as jnp
from jax import lax
from jax.experimental import pallas as pl
from jax.experimental.pallas import tpu as pltpu

F32 = jnp.float32
BF16 = jnp.bfloat16

D = 1024
GRID_W = 64
NORM_EPS = 1e-6
N_MIXERS = 3
ROW_TILE = 256
RET_HEADS = 4
RET_DK = 256
RET_DV = 512
ROPE_BASE = 10000.0
GM_WIDTH = 2 * D
GM_GROUPS = 8
GM_CHUNK = 128
RW_HEAD = 64
RW_LORA = 64
RW_LNX_EPS = 64e-5
WKV_CHUNK = 64
WKV_LANES = 256
WKV_GROUP = WKV_LANES // RW_HEAD
VMEM_LIMIT = 56 * 1024 * 1024

_NT = (((1,), (1,)), ((), ()))
_TN = (((0,), (0,)), ((), ()))


def _params(*sem):
    return pltpu.CompilerParams(dimension_semantics=sem, vmem_limit_bytes=VMEM_LIMIT)


def _silu(x):
    return x / (1.0 + jnp.exp(-x))


def _bdot(a, b, dims=None):
    a = a.astype(BF16)
    b = b.astype(BF16)
    if dims is None:
        return jnp.dot(a, b, preferred_element_type=F32)
    return lax.dot_general(a, b, dims, preferred_element_type=F32)


def _split3(x):
    hi = x.astype(BF16)
    r1 = x - hi.astype(F32)
    mid = r1.astype(BF16)
    lo = (r1 - mid.astype(F32)).astype(BF16)
    return hi, mid, lo


def _norm_mod(x, g, mod):
    ms = jnp.mean(x * x, axis=-1, keepdims=True)
    y = x * lax.rsqrt(ms + NORM_EPS) * g
    return y * (1.0 + mod[:, D:2 * D]) + mod[:, :D]


def _mod_spec(layer, n_batch, tile_of_step):
    def index(b, s):
        t = tile_of_step(s)
        return (layer, jnp.where(t == 0, n_batch, b), 0, 0)
    return pl.BlockSpec((None, None, 1, 3 * D), index)


def _identity(s):
    return s


def _mod_kernel(c_ref, w_ref, b_ref, o_ref):
    s = _silu(c_ref[...])
    o_ref[...] = jnp.dot(s, w_ref[...], preferred_element_type=F32,
                         precision=lax.Precision.HIGHEST) + b_ref[...]


def _modulation(cvec, ada_w, ada_b):
    depth = ada_w.shape[0]
    return pl.pallas_call(
        _mod_kernel,
        grid=(depth, 3),
        in_specs=[pl.BlockSpec((8, D), lambda i, j: (0, 0)),
                  pl.BlockSpec((None, D, D), lambda i, j: (i, 0, j)),
                  pl.BlockSpec((None, 1, D), lambda i, j: (i, 0, j))],
        out_specs=pl.BlockSpec((None, 8, D), lambda i, j: (i, 0, j)),
        out_shape=jax.ShapeDtypeStruct((depth, 8, 3 * D), F32),
        compiler_params=_params("arbitrary", "arbitrary"),
    )(cvec, ada_w, ada_b.reshape(depth, 1, 3 * D))


PROJ_COLS = 512


def _rope_table(ctx_len, seq_len):
    n_freq = RET_DK // 4
    freqs = ROPE_BASE ** (-jnp.arange(n_freq, dtype=F32) / n_freq)
    t = jnp.arange(seq_len, dtype=jnp.int32)
    row, col = (t // GRID_W).astype(F32), (t % GRID_W).astype(F32)
    ang = jnp.concatenate([row[:, None] * freqs, col[:, None] * freqs], axis=-1)
    ang = jnp.concatenate([jnp.zeros((ctx_len, RET_DK // 2), F32), ang], axis=0)
    return jnp.concatenate([jnp.cos(ang), jnp.sin(ang)], axis=-1)


RET_ROW_LANES = 128


def _ret_table_kernel(dl_ref, mask_ref, row_ref, cd_ref):
    c = ROW_TILE
    i = lax.broadcasted_iota(jnp.int32, (c, c), 0).astype(F32)
    j = lax.broadcasted_iota(jnp.int32, (c, c), 1).astype(F32)
    ir = i[:, :RET_ROW_LANES]

    def log_sigmoid(x):
        return jnp.minimum(x, 0.0) - jnp.log(1.0 + jnp.exp(-jnp.abs(x)))

    for h in range(RET_HEADS):
        lg_f = log_sigmoid(jnp.full((c, c), dl_ref[0, h], F32))
        lg_b = log_sigmoid(jnp.full((c, c), dl_ref[1, h], F32))
        fwd = jnp.where(i >= j, jnp.exp(jnp.maximum(i - j, 0.0) * lg_f), 0.0)
        bwd = jnp.where(j >= i, jnp.exp(jnp.maximum(j - i, 0.0) * lg_b), 0.0)
        mask_ref[h] = fwd + bwd
        lf = log_sigmoid(jnp.full((c, RET_ROW_LANES), dl_ref[0, h], F32))
        lb = log_sigmoid(jnp.full((c, RET_ROW_LANES), dl_ref[1, h], F32))
        row_ref[h, 0] = jnp.exp((ir + 1.0) * lf)
        row_ref[h, 1] = jnp.exp((c - 1.0 - ir) * lf)
        row_ref[h, 2] = jnp.exp((c - ir) * lb)
        row_ref[h, 3] = jnp.exp(ir * lb)
        cd_ref[h, 0] = jnp.exp(c * log_sigmoid(jnp.full((8, RET_DV), dl_ref[0, h], F32)))
        cd_ref[h, 1] = jnp.exp(c * log_sigmoid(jnp.full((8, RET_DV), dl_ref[1, h], F32)))


def _ret_tables(decay_logit):
    c = ROW_TILE
    return pl.pallas_call(
        _ret_table_kernel,
        in_specs=[pl.BlockSpec(memory_space=pltpu.SMEM)],
        out_shape=(jax.ShapeDtypeStruct((RET_HEADS, c, c), F32),
                   jax.ShapeDtypeStruct((RET_HEADS, 4, c, RET_ROW_LANES), F32),
                   jax.ShapeDtypeStruct((RET_HEADS, 2, 8, RET_DV), F32)),
    )(decay_logit)


def _lanes(t, width):
    return jnp.concatenate([t] * (width // RET_ROW_LANES), axis=1)


def _ret_heads(q_ref, k_ref, v_ref, k_col=0, v_col=0):
    return [(q_ref[:, h * RET_DK:(h + 1) * RET_DK], k_ref[:, k_col + h * RET_DK:k_col + (h + 1) * RET_DK],
             v_ref[:, v_col + h * RET_DV:v_col + (h + 1) * RET_DV]) for h in range(RET_HEADS)]


def _ret_fwd_kernel(x_ref, mod_ref, g_ref, w_ref, cs_ref, mask_ref, row_ref, cd_ref, p_ref, o_ref, s_ref):
    @pl.when(pl.program_id(1) == 0)
    def _():
        s_ref[...] = jnp.zeros_like(s_ref)

    h_in = _norm_mod(x_ref[...], g_ref[...], mod_ref[...]).astype(BF16)
    half = RET_DK // 2
    k_col = RET_HEADS * RET_DK
    v_col = 2 * k_col
    z_col = v_col + RET_HEADS * RET_DV
    cos = cs_ref[:, :half]
    sin = cs_ref[:, half:]
    for c0 in range(0, v_col, RET_DK):
        acc = jnp.dot(h_in, w_ref[:, c0:c0 + RET_DK], preferred_element_type=F32)
        t1, t2 = acc[:, :half], acc[:, half:]
        p_ref[:, c0:c0 + half] = (t1 * cos - t2 * sin).astype(BF16)
        p_ref[:, c0 + half:c0 + RET_DK] = (t1 * sin + t2 * cos).astype(BF16)
    for c0 in range(v_col, z_col, PROJ_COLS):
        p_ref[:, c0:c0 + PROJ_COLS] = jnp.dot(h_in, w_ref[:, c0:c0 + PROJ_COLS],
                                              preferred_element_type=F32).astype(BF16)

    def gate_cols(h):
        c0 = z_col + h * RET_DV
        acc = jnp.dot(h_in, w_ref[:, c0:c0 + RET_DV], preferred_element_type=F32)
        p_ref[:, c0:c0 + RET_DV] = _silu(acc).astype(BF16)

    qkv = _ret_heads(p_ref, p_ref, p_ref, k_col, v_col)
    sc = {}

    def scores(h):
        q, k, _ = qkv[h]
        sc[h] = lax.dot_general(q, k, _NT, preferred_element_type=F32)

    def finish(h):
        q, k, v = qkv[h]
        s = s_ref[h]
        qs = _bdot(q, s)
        kd = k.astype(F32) * _lanes(row_ref[h, 1], RET_DK)
        s_ref[h] = s * cd_ref[h, 0, 0:1, :] + _bdot(kd, v, _TN)
        o = _bdot(sc[h] * mask_ref[h], v)
        o_ref[:, h * RET_DV:(h + 1) * RET_DV] = (o + qs * _lanes(row_ref[h, 0], RET_DV)).astype(BF16)

    scores(0)
    for h in range(RET_HEADS):
        if h + 1 < RET_HEADS:
            scores(h + 1)
        gate_cols(h)
        finish(h)


def _ret_bwd_kernel(q_ref, k_ref, v_ref, z_ref, op_ref, x_ref, mod_ref, wo_ref, row_ref, cd_ref, fg_ref,
                    xo_ref, s_ref, *, final):
    @pl.when(pl.program_id(1) == 0)
    def _():
        s_ref[...] = jnp.zeros_like(s_ref)

    qkv = _ret_heads(q_ref, k_ref, v_ref)
    gz = {}

    def inter(h):
        q, k, v = qkv[h]
        vv = slice(h * RET_DV, (h + 1) * RET_DV)
        s = s_ref[h]
        o = op_ref[:, vv].astype(F32) + _bdot(q, s) * _lanes(row_ref[h, 2], RET_DV)
        kd = k.astype(F32) * _lanes(row_ref[h, 3], RET_DK)
        s_ref[h] = s * cd_ref[h, 1, 0:1, :] + _bdot(kd, v, _TN)
        o = o * lax.rsqrt(jnp.mean(o * o, axis=-1, keepdims=True) + NORM_EPS)
        gz[h] = (o * z_ref[:, vv].astype(F32)).astype(BF16)

    acc = jnp.zeros((ROW_TILE, D), F32)
    inter(0)
    for h in range(RET_HEADS):
        if h + 1 < RET_HEADS:
            inter(h + 1)
        acc = acc + jnp.dot(gz[h], wo_ref[h * RET_DV:(h + 1) * RET_DV, :], preferred_element_type=F32)
    xn = x_ref[...] + mod_ref[:, 2 * D:] * acc
    if final:
        xn = xn * lax.rsqrt(jnp.mean(xn * xn, axis=-1, keepdims=True) + NORM_EPS) * fg_ref[...]
    xo_ref[...] = xn


def _bwd_tile(tiles):
    return lambda s: jnp.where(s == 0, 0, tiles - s)


def _retention(xc, mod4, g, w_in, rope, w_out, tables, layer, final_g):
    n_batch, lc, _ = xc.shape
    tiles = lc // ROW_TILE
    width = RET_HEADS * RET_DV
    state = pltpu.VMEM((RET_HEADS, RET_DK, RET_DV), F32)
    mask, row, cd = tables
    mask_spec = pl.BlockSpec(mask.shape, lambda b, s: (0, 0, 0))
    row_spec = pl.BlockSpec(row.shape, lambda b, s: (0, 0, 0, 0))
    cd_spec = pl.BlockSpec(cd.shape, lambda b, s: (0, 0, 0, 0))

    def col_spec(width_, col, tile_of_step):
        return pl.BlockSpec((None, ROW_TILE, width_), lambda b, s: (b, tile_of_step(s), col))

    n_proj = w_in.shape[1]
    proj, o_part = pl.pallas_call(
        _ret_fwd_kernel,
        grid=(n_batch, tiles),
        in_specs=[col_spec(D, 0, _identity), _mod_spec(layer, n_batch, _identity),
                  pl.BlockSpec((1, D), lambda b, s: (0, 0)), pl.BlockSpec((D, n_proj), lambda b, s: (0, 0)),
                  pl.BlockSpec((ROW_TILE, RET_DK), lambda b, s: (s, 0)), mask_spec, row_spec, cd_spec],
        out_specs=[col_spec(n_proj, 0, _identity), col_spec(width, 0, _identity)],
        out_shape=[jax.ShapeDtypeStruct((n_batch, lc, n_proj), BF16),
                   jax.ShapeDtypeStruct((n_batch, lc, width), BF16)],
        scratch_shapes=[state],
        compiler_params=_params("parallel", "arbitrary"),
    )(xc, mod4, g.reshape(1, D), w_in, rope, mask, row, cd)

    bt = _bwd_tile(tiles)
    final = final_g is not None
    if final:
        ctx_tiles = 1
        out_spec = col_spec(D, 0, lambda s: bt(jnp.maximum(s, 1)) - ctx_tiles)
        out_shape = jax.ShapeDtypeStruct((n_batch, lc - ctx_tiles * ROW_TILE, D), F32)
        fg = final_g.reshape(1, D)
    else:
        out_spec = col_spec(D, 0, bt)
        out_shape = jax.ShapeDtypeStruct(xc.shape, F32)
        fg = jnp.ones((1, D), F32)
    return pl.pallas_call(
        functools.partial(_ret_bwd_kernel, final=final),
        grid=(n_batch, tiles),
        in_specs=[col_spec(RET_HEADS * RET_DK, 0, bt), col_spec(RET_HEADS * RET_DK, 1, bt),
                  col_spec(width, 1, bt), col_spec(width, 2, bt), col_spec(width, 0, bt),
                  col_spec(D, 0, bt), _mod_spec(layer, n_batch, bt),
                  pl.BlockSpec((width, D), lambda b, s: (0, 0)), row_spec, cd_spec,
                  pl.BlockSpec((1, D), lambda b, s: (0, 0))],
        out_specs=out_spec,
        out_shape=out_shape,
        scratch_shapes=[state],
        compiler_params=_params("parallel", "arbitrary"),
    )(proj, proj, proj, proj, o_part, xc, mod4, w_out, row, cd, fg)


def _gm_kernel(x_ref, mod_ref, g_ref, w_ref, vg_ref, ws_ref, bs_ref, wo_ref, xo_ref, v_ref):
    mod = mod_ref[...]
    h = _norm_mod(x_ref[...], g_ref[...], mod).astype(BF16)
    for c0 in range(0, GM_WIDTH, PROJ_COLS):
        v_ref[:, c0:c0 + PROJ_COLS] = jnp.dot(h, w_ref[:, GM_WIDTH + c0:GM_WIDTH + c0 + PROJ_COLS],
                                              preferred_element_type=F32)
    v = v_ref[...]
    v = v - jnp.mean(v, axis=-1, keepdims=True)
    vn = (v * lax.rsqrt(jnp.mean(v * v, axis=-1, keepdims=True) + NORM_EPS) * vg_ref[...]).astype(BF16)

    c = GM_CHUNK
    gw = GM_WIDTH // GM_GROUPS
    gated = {}

    def group(g):
        cols = slice(g * gw, (g + 1) * gw)
        u = jnp.dot(h, w_ref[:, cols], preferred_element_type=F32)
        z = jnp.dot(h, w_ref[:, 2 * GM_WIDTH + g * gw:2 * GM_WIDTH + (g + 1) * gw], preferred_element_type=F32)
        mixed = jnp.concatenate([jnp.dot(ws_ref[g], vn[ci * c:(ci + 1) * c, cols], preferred_element_type=F32)
                                 for ci in range(ROW_TILE // c)], axis=0)
        mixed = mixed + jnp.concatenate([bs_ref[:, g:g + 1]] * (ROW_TILE // c), axis=0)
        gated[g] = (u * mixed * _silu(z)).astype(BF16)

    acc = jnp.zeros((ROW_TILE, D), F32)
    group(0)
    for g in range(GM_GROUPS):
        if g + 1 < GM_GROUPS:
            group(g + 1)
        acc = acc + jnp.dot(gated[g], wo_ref[g * gw:(g + 1) * gw, :], preferred_element_type=F32)
    xo_ref[...] = x_ref[...] + mod[:, 2 * D:] * acc


def _gmlp(xc, mod4, g, w_in, vnorm_g, w_s, b_s, w_out, layer):
    n_batch, lc, _ = xc.shape
    tiles = lc // ROW_TILE
    tile_spec = pl.BlockSpec((None, ROW_TILE, D), lambda b, t: (b, t, 0))

    def full(a):
        return pl.BlockSpec(a.shape, lambda b, t: (0,) * a.ndim)

    return pl.pallas_call(
        _gm_kernel,
        grid=(n_batch, tiles),
        in_specs=[tile_spec, _mod_spec(layer, n_batch, _identity), full(g), full(w_in), full(vnorm_g), full(w_s),
                  full(b_s), full(w_out)],
        out_specs=tile_spec,
        out_shape=jax.ShapeDtypeStruct(xc.shape, F32),
        scratch_shapes=[pltpu.VMEM((ROW_TILE, GM_WIDTH), F32)],
        compiler_params=_params("parallel", "parallel"),
    )(xc, mod4, g, w_in, vnorm_g, w_s, b_s, w_out)


def _head_sum(x, e_ref):
    outs = []
    e = e_ref[...]
    for g in range(x.shape[-1] // WKV_LANES):
        xs = x[:, g * WKV_LANES:(g + 1) * WKV_LANES]
        hi = xs.astype(BF16)
        lo = (xs - hi.astype(F32)).astype(BF16)
        outs.append(jnp.dot(hi, e, preferred_element_type=F32) + jnp.dot(lo, e, preferred_element_type=F32))
    return jnp.concatenate(outs, axis=1)


def _rw_prep_kernel(xp_ref, x_ref, xn_ref, mod_ref, g_ref, mu_ref, wm_ref, w1_ref, w2_ref, a1_ref, a2_ref,
                    vec_ref, e_ref, r_ref, v_ref, kk_ref, k_ref, b_ref, ld_ref, bonus_ref, z_ref,
                    *, grid_rows):
    t = pl.program_id(1)
    is_ctx = t == 0
    g, mod = g_ref[...], mod_ref[...]
    h = _norm_mod(x_ref[...], g, mod)
    h_up = _norm_mod(xp_ref[...], g, mod)
    h_dn = _norm_mod(xn_ref[...], g, mod)
    n = ROW_TILE
    prev = pltpu.roll(h, 1, 0)
    nxt = pltpu.roll(h, n - 1, 0)
    ext = jnp.concatenate([h_up, h, h_dn], axis=0)
    up, down = ext[:n], ext[2 * GRID_W:]

    lane_q = lax.broadcasted_iota(jnp.int32, (n, D), 1) // (D // 4)
    i = lax.broadcasted_iota(jnp.int32, (n, D), 0)
    grow = (t - 1) * (n // GRID_W) + i // GRID_W
    pos = jnp.where(is_ctx, i, i % GRID_W)
    last_pos = jnp.where(is_ctx, n - 1, GRID_W - 1)
    prev_lanes = jnp.where(is_ctx, 2, 1)
    next_lanes = jnp.where(is_ctx, 4, 2)
    prev = jnp.where(pos > 0, prev, 0.0)
    nxt = jnp.where(pos < last_pos, nxt, 0.0)
    up = jnp.where(grow > 0, up, 0.0)
    down = jnp.where(grow < grid_rows - 1, down, 0.0)
    shifted = jnp.where(lane_q < prev_lanes, prev,
                        jnp.where(lane_q < next_lanes, nxt, jnp.where(lane_q == 2, up, down)))
    xx = shifted - h

    hb, xb = h.astype(BF16), xx.astype(BF16)

    def mix(p):
        return hb + xb * mu_ref[p:p + 1, :].astype(BF16)

    vec = vec_ref[...]
    r = jnp.dot(mix(0), wm_ref[0], preferred_element_type=F32)
    k = jnp.dot(mix(2), wm_ref[1], preferred_element_type=F32)
    v = jnp.dot(mix(3), wm_ref[2], preferred_element_type=F32)
    z_ref[...] = jnp.dot(mix(5), wm_ref[3], preferred_element_type=F32).astype(BF16)
    tw = jnp.tanh(jnp.dot(mix(1), w1_ref[...], preferred_element_type=F32))
    ta = jnp.dot(mix(4), a1_ref[...], preferred_element_type=F32)

    kk = k * vec[4:5]
    kk = kk / jnp.maximum(jnp.sqrt(_head_sum(kk * kk, e_ref)), 1e-12)
    r_ref[...] = r
    v_ref[...] = v
    kk_ref[...] = kk
    bonus = jnp.zeros_like(r)
    for d in range(2):
        lw = vec[d:d + 1] + _bdot(tw, w2_ref[d])
        ld_ref[d] = -math.exp(-0.5) / (1.0 + jnp.exp(-lw))
        a = 1.0 / (1.0 + jnp.exp(-(vec[2 + d:3 + d] + _bdot(ta, a2_ref[d]))))
        kd = k * (1.0 + (a - 1.0) * vec[5:6])
        k_ref[d] = kd
        b_ref[d] = kk * a
        bonus = bonus + r * kd * vec[6:7]
    bonus_ref[...] = (_head_sum(bonus, e_ref) * v).astype(BF16)


def _rw_prepare(xc, mod4, g, mu, wm, w1, w2, a1, a2, vec, e, layer, grid_rows):
    n_batch, lc, _ = xc.shape
    tiles = lc // ROW_TILE
    per_tile = ROW_TILE // GRID_W
    last = lc // GRID_W - 1

    def full(a):
        return pl.BlockSpec(a.shape, lambda b, t: (0,) * a.ndim)

    tile_spec = pl.BlockSpec((None, ROW_TILE, D), lambda b, t: (b, t, 0))
    dir_spec = pl.BlockSpec((2, None, ROW_TILE, D), lambda b, t: (0, b, t, 0))
    one = jax.ShapeDtypeStruct((n_batch, lc, D), F32)
    two = jax.ShapeDtypeStruct((2, n_batch, lc, D), F32)
    half = jax.ShapeDtypeStruct((n_batch, lc, D), BF16)
    return pl.pallas_call(
        functools.partial(_rw_prep_kernel, grid_rows=grid_rows),
        grid=(n_batch, tiles),
        in_specs=[pl.BlockSpec((None, GRID_W, D), lambda b, t: (b, jnp.maximum(t * per_tile - 1, 0), 0)),
                  tile_spec,
                  pl.BlockSpec((None, GRID_W, D), lambda b, t: (b, jnp.minimum((t + 1) * per_tile, last), 0)),
                  _mod_spec(layer, n_batch, _identity), full(g), full(mu), full(wm), full(w1), full(w2),
                  full(a1), full(a2), full(vec), full(e)],
        out_specs=[tile_spec, tile_spec, tile_spec, dir_spec, dir_spec, dir_spec, tile_spec, tile_spec],
        out_shape=[one, one, one, two, two, two, half, half],
        compiler_params=_params("parallel", "parallel"),
    )(xc, xc, xc, mod4, g, mu, wm, w1, w2, a1, a2, vec, e)


def _wkv_masks(reverse):
    c, hg = WKV_CHUNK, WKV_GROUP
    t = lax.broadcasted_iota(jnp.int32, (c, hg * c), 0)
    s = lax.broadcasted_iota(jnp.int32, (c, hg * c), 1) % c
    strict = jnp.where((s > t) if reverse else (s < t), 1.0, 0.0)
    incl = jnp.where((s >= t) if reverse else (s <= t), 1.0, 0.0)
    eye = jnp.where(s == t, 1.0, 0.0)
    return strict, jnp.concatenate([incl, incl], axis=1), eye, jnp.concatenate([incl[:, :c]] * 3, axis=1).astype(BF16)


def _wkv_chunks(chains, m4, bd, bd_state):
    c, w, hg = WKV_CHUNK, WKV_LANES, WKV_GROUP
    n = range(len(chains))
    refs, rows, masks, reverse = zip(*[(ch["refs"], (ch["rows"], ch["lanes"]), ch["masks"], ch["reverse"])
                                       for ch in chains])
    strict, incl2, eye, cum_mat = zip(*masks)

    def dot(a, b, dims=None):
        if dims is None:
            return jnp.dot(a, b, preferred_element_type=F32)
        return lax.dot_general(a, b, dims, preferred_element_type=F32)

    def stack(x):
        return jnp.where(jnp.concatenate([m4] * (x.shape[1] // w), axis=1), jnp.concatenate([x] * hg, axis=0), 0)

    def blockdiag(x):
        return jnp.where(bd, jnp.concatenate([x.astype(BF16)] * hg, axis=0), 0)

    ld = [refs[i][5][rows[i]] for i in n]
    cum = [dot(cum_mat[i], jnp.concatenate(_split3(ld[i]), axis=0)) for i in n]
    tot = [cum[i][0:1] if reverse[i] else cum[i][c - 1:c] for i in n]
    e_out = [jnp.exp(-cum[i]) for i in n]
    vb = [refs[i][4][rows[i]].astype(BF16) for i in n]
    rt = [(refs[i][0][rows[i]] * jnp.exp(cum[i])).astype(BF16) for i in n]
    at = [(-refs[i][1][rows[i]] * jnp.exp(cum[i] - ld[i])).astype(BF16) for i in n]
    kt = [(refs[i][3][rows[i]] * e_out[i]).astype(BF16) for i in n]
    bt = [(refs[i][2][rows[i]] * e_out[i]).astype(BF16) for i in n]

    sc = [dot(jnp.concatenate([at[i], rt[i]], axis=0), jnp.concatenate([stack(kt[i]), stack(bt[i])], axis=0), _NT)
          for i in n]
    a_ak = [(sc[i][:c, :w] * strict[i]).astype(BF16) for i in n]
    p = [sc[i][:c, w:] * strict[i] for i in n]
    a_r = [(sc[i][c:] * incl2[i]).astype(BF16) for i in n]
    tinv = [eye[i] + p[i] for i in n]
    for _ in range(c.bit_length() - 2):
        p = [dot(p[i].astype(BF16), blockdiag(p[i])) for i in n]
        tinv = [tinv[i] + dot(p[i].astype(BF16), blockdiag(tinv[i])) for i in n]
    sv = [stack(vb[i]) for i in n]
    akv = [dot(a_ak[i], sv[i]).astype(BF16) for i in n]
    tx = [dot(tinv[i].astype(BF16), stack(jnp.concatenate([at[i], akv[i]], axis=1))) for i in n]

    ht = [ch["ht"][ch["slot"]] for ch in chains]
    wrh = [dot(jnp.concatenate([tx[i][:, :w].astype(BF16), rt[i]], axis=0), ht[i].astype(BF16), _NT) for i in n]
    ub = [(wrh[i][:c] + tx[i][:, w:]).astype(BF16) for i in n]
    y = [wrh[i][c:] + dot(a_r[i], jnp.concatenate([sv[i], stack(ub[i])], axis=0)) for i in n]
    e_end = [jnp.exp(tot[i] - cum[i]) for i in n]
    kb = [jnp.concatenate([(refs[i][3][rows[i]] * e_end[i]).astype(BF16),
                           (refs[i][2][rows[i]] * e_end[i]).astype(BF16)], axis=0) for i in n]
    upd = [dot(jnp.concatenate([vb[i], ub[i]], axis=0), kb[i], _TN) for i in n]
    for i, ch in enumerate(chains):
        ch["y"][rows[i]] = y[i].astype(BF16)
        ch["ht"][ch["slot"]] = jnp.where(bd_state, ht[i] * jnp.exp(tot[i]) + upd[i], 0.0)


def _wkv_kernel(rf, kkf, bf, kf, vf, ldf, rb, kkb, bb, kb, vb, ldb, yf_ref, yb_ref, ht_ref):
    @pl.when(pl.program_id(1) == 0)
    def _():
        ht_ref[...] = jnp.zeros_like(ht_ref)

    c, w, hg = WKV_CHUNK, WKV_LANES, WKV_GROUP
    n_chunks = ROW_TILE // c
    lane_head = lax.broadcasted_iota(jnp.int32, (hg * c, w), 1) // RW_HEAD
    row_head = lax.broadcasted_iota(jnp.int32, (hg * c, w), 0) // c
    m4 = lane_head == row_head
    bd = lax.broadcasted_iota(jnp.int32, (hg * c, hg * c), 1) // c == row_head
    bd_state = (lax.broadcasted_iota(jnp.int32, (w, w), 0) // RW_HEAD
                == lax.broadcasted_iota(jnp.int32, (w, w), 1) // RW_HEAD)
    masks_f, masks_b = _wkv_masks(False), _wkv_masks(True)
    groups = D // w

    def body(step, carry):
        rows_f = pl.ds(pl.multiple_of(step * c, c), c)
        rows_b = pl.ds(pl.multiple_of((n_chunks - 1 - step) * c, c), c)
        chains = []
        for g in range(groups):
            lanes = slice(g * w, (g + 1) * w)
            chains.append(dict(refs=(rf, kkf, bf, kf, vf, ldf), rows=rows_f, lanes=lanes, masks=masks_f,
                               reverse=False, ht=ht_ref, slot=g, y=yf_ref))
            chains.append(dict(refs=(rb, kkb, bb, kb, vb, ldb), rows=rows_b, lanes=lanes, masks=masks_b,
                               reverse=True, ht=ht_ref, slot=groups + g, y=yb_ref))
        _wkv_chunks(chains, m4, bd, bd_state)
        return carry

    lax.fori_loop(0, n_chunks, body, 0)


def _wkv(r, kk, b, k, v, ld):
    n_batch, lc, _ = r.shape
    tiles = lc // ROW_TILE
    bt = _bwd_tile(tiles)

    def specs(tile_of, direction):
        one = pl.BlockSpec((None, ROW_TILE, D), lambda bb, s: (bb, tile_of(s), 0))
        two = pl.BlockSpec((None, None, ROW_TILE, D), lambda bb, s: (direction, bb, tile_of(s), 0))
        return one, two

    f1, f2 = specs(_identity, 0)
    b1, b2 = specs(bt, 1)
    return pl.pallas_call(
        _wkv_kernel,
        grid=(n_batch, tiles),
        in_specs=[f1, f1, f2, f2, f1, f2, b1, b1, b2, b2, b1, b2],
        out_specs=[f1, b1],
        out_shape=[jax.ShapeDtypeStruct(r.shape, BF16)] * 2,
        scratch_shapes=[pltpu.VMEM((2 * (D // WKV_LANES), WKV_LANES, WKV_LANES), F32)],
        compiler_params=_params("parallel", "arbitrary"),
    )(r, kk, b, k, v, ld, r, kk, b, k, v, ld)


def _rw_out_kernel(yf_ref, yb_ref, bonus_ref, z_ref, x_ref, mod_ref, vec_ref, e_ref, wo_ref, xo_ref):
    y = yf_ref[...].astype(F32) + yb_ref[...].astype(F32)
    inv = 1.0 / RW_HEAD
    yc = y - _head_sum(y, e_ref) * inv
    yn = yc * lax.rsqrt(_head_sum(yc * yc, e_ref) * inv + RW_LNX_EPS)
    yn = yn * vec_ref[0:1] + vec_ref[1:2]
    o = (yn + bonus_ref[...].astype(F32)) * _silu(z_ref[...].astype(F32))
    xo_ref[...] = x_ref[...] + mod_ref[:, 2 * D:] * _bdot(o, wo_ref[...])


def _rw_output(xc, y_f, y_b, bonus, z, mod4, vec, e, w_out, layer):
    n_batch, lc, _ = xc.shape
    tiles = lc // ROW_TILE
    tile_spec = pl.BlockSpec((None, ROW_TILE, D), lambda b, t: (b, t, 0))

    def full(a):
        return pl.BlockSpec(a.shape, lambda b, t: (0,) * a.ndim)

    return pl.pallas_call(
        _rw_out_kernel,
        grid=(n_batch, tiles),
        in_specs=[tile_spec, tile_spec, tile_spec, tile_spec, tile_spec, _mod_spec(layer, n_batch, _identity),
                  full(vec), full(e), full(w_out)],
        out_specs=tile_spec,
        out_shape=jax.ShapeDtypeStruct(xc.shape, F32),
        compiler_params=_params("parallel", "parallel"),
    )(y_f, y_b, bonus, z, xc, mod4, vec, e, w_out)


def _rwkv(xc, mod4, g, mu, w_rkvg, w0, w1, w2, a0, a1, a2, k_k, k_a, r_k, lnx_g, lnx_b, w_out, layer, grid_rows):
    zeros = jnp.zeros((RW_LORA, D), F32)
    w1c = jnp.concatenate([w1[0], w1[1]], axis=1).astype(BF16)
    a1c = jnp.concatenate([a1[0], a1[1]], axis=1).astype(BF16)
    w2p = jnp.stack([jnp.concatenate([w2[0], zeros]), jnp.concatenate([zeros, w2[1]])]).astype(BF16)
    a2p = jnp.stack([jnp.concatenate([a2[0], zeros]), jnp.concatenate([zeros, a2[1]])]).astype(BF16)
    zrow = jnp.zeros((D,), F32)
    vec = jnp.stack([w0[0], w0[1], a0[0], a0[1], k_k, k_a, r_k.reshape(D), zrow])
    lane = jnp.arange(WKV_LANES) // RW_HEAD
    e = (lane[:, None] == lane[None, :]).astype(BF16)
    r, v, kk, k, b, ld, bonus, z = _rw_prepare(xc, mod4, g.reshape(1, D), mu, w_rkvg.astype(BF16), w1c, w2p,
                                               a1c, a2p, vec, e, layer, grid_rows)
    y_f, y_b = _wkv(r, kk, b, k, v, ld)
    vec_o = jnp.stack([lnx_g, lnx_b] + [zrow] * 6)
    return _rw_output(xc, y_f, y_b, bonus, z, mod4, vec_o, e, w_out.astype(BF16), layer)


def kernel(x, c, ctx, c_ctx, ada_w, ada_b, norm_g, final_g, ret_w_in, ret_decay, ret_w_out, gm_w_in, gm_vnorm_g,
           gm_w_s, gm_b_s, gm_w_out, rw_mu, rw_w_rkvg, rw_w0, rw_w1, rw_w2, rw_a0, rw_a1, rw_a2, rw_k_k, rw_k_a,
           rw_r_k, rw_lnx_g, rw_lnx_b, rw_w_out):
    n_batch, seq_len, _ = x.shape
    ctx_len = ctx.shape[1]
    depth = ada_w.shape[0]
    assert ctx_len == ROW_TILE and seq_len % ROW_TILE == 0 and n_batch < 8

    xc = jnp.concatenate([ctx, x], axis=1)
    cvec = jnp.zeros((8, D), F32).at[:n_batch].set(c).at[n_batch].set(c_ctx)
    mod4 = _modulation(cvec, ada_w, ada_b).reshape(depth, 8, 1, 3 * D)
    rope = _rope_table(ctx_len, seq_len)
    k_scale = jnp.concatenate([jnp.ones((RET_HEADS * RET_DK,), F32),
                               jnp.full((RET_HEADS * RET_DK,), RET_DK ** -0.5, F32),
                               jnp.ones((2 * RET_HEADS * RET_DV,), F32)])

    for i in range(depth):
        kind, j = i % N_MIXERS, i // N_MIXERS
        if kind == 0:
            w_in = (ret_w_in[j] * k_scale).astype(BF16)
            xc = _retention(xc, mod4, norm_g[i], w_in, rope, ret_w_out[j].astype(BF16), _ret_tables(ret_decay[j]), i,
                            final_g if i == depth - 1 else None)
        elif kind == 1:
            xc = _gmlp(xc, mod4, norm_g[i].reshape(1, D), gm_w_in[j].astype(BF16), gm_vnorm_g[j].reshape(1, GM_WIDTH),
                       gm_w_s[j].astype(BF16), gm_b_s[j].T, gm_w_out[j].astype(BF16), i)
        else:
            xc = _rwkv(xc, mod4, norm_g[i], rw_mu[j], rw_w_rkvg[j], rw_w0[j], rw_w1[j], rw_w2[j], rw_a0[j],
                       rw_a1[j], rw_a2[j], rw_k_k[j], rw_k_a[j], rw_r_k[j], rw_lnx_g[j], rw_lnx_b[j],
                       rw_w_out[j], i, seq_len // GRID_W)
    assert (depth - 1) % N_MIXERS == 0, "the final norm is fused into a retention layer"
    return xc
```

```python
import functools
import math

import jax
import jax.numpy as jnp
from jax import lax
from jax.experimental import pallas as pl
from jax.experimental.pallas import tpu as pltpu

F32 = jnp.float32
BF16 = jnp.bfloat16

D = 1024
GRID_W = 64
NORM_EPS = 1e-6
N_MIXERS = 3
ROW_TILE = 256
RET_HEADS = 4
RET_DK = 256
RET_DV = 512
ROPE_BASE = 10000.0
GM_WIDTH = 2 * D
GM_GROUPS = 8
GM_CHUNK = 128
RW_HEAD = 64
RW_LORA = 64
RW_LNX_EPS = 64e-5
WKV_CHUNK = 64
WKV_LANES = 256
WKV_GROUP = WKV_LANES // RW_HEAD
WKV_SETUP_STAGES, WKV_INVERSE_STAGES, WKV_APPLY_STAGES = 2, 10, 5
WKV_SETUP_SLOTS = (10, 12)
VMEM_LIMIT = 56 * 1024 * 1024

_NT = (((1,), (1,)), ((), ()))
_TN = (((0,), (0,)), ((), ()))


def _params(*sem):
    return pltpu.CompilerParams(dimension_semantics=sem, vmem_limit_bytes=VMEM_LIMIT)


def _silu(x):
    return x / (1.0 + jnp.exp(-x))


def _bdot(a, b, dims=None):
    a = a.astype(BF16)
    b = b.astype(BF16)
    if dims is None:
        return jnp.dot(a, b, preferred_element_type=F32)
    return lax.dot_general(a, b, dims, preferred_element_type=F32)


def _split3(x):
    hi = x.astype(BF16)
    r1 = x - hi.astype(F32)
    mid = r1.astype(BF16)
    lo = (r1 - mid.astype(F32)).astype(BF16)
    return hi, mid, lo


def _norm_mod(x, g, mod):
    ms = jnp.mean(x * x, axis=-1, keepdims=True)
    y = x * lax.rsqrt(ms + NORM_EPS) * g
    return y * (1.0 + mod[:, D:2 * D]) + mod[:, :D]


def _mod_spec(layer, n_batch, tile_of_step):
    def index(b, s):
        t = tile_of_step(s)
        return (layer, jnp.where(t == 0, n_batch, b), 0, 0)
    return pl.BlockSpec((None, None, 1, 3 * D), index)


def _identity(s):
    return s


def _mod_kernel(c_ref, w_ref, b_ref, o_ref):
    s = _silu(c_ref[...])
    o_ref[...] = jnp.dot(s, w_ref[...], preferred_element_type=F32,
                         precision=lax.Precision.HIGHEST) + b_ref[...]


def _modulation(cvec, ada_w, ada_b):
    depth = ada_w.shape[0]
    return pl.pallas_call(
        _mod_kernel,
        grid=(depth, 3),
        in_specs=[pl.BlockSpec((8, D), lambda i, j: (0, 0)),
                  pl.BlockSpec((None, D, D), lambda i, j: (i, 0, j)),
                  pl.BlockSpec((None, 1, D), lambda i, j: (i, 0, j))],
        out_specs=pl.BlockSpec((None, 8, D), lambda i, j: (i, 0, j)),
        out_shape=jax.ShapeDtypeStruct((depth, 8, 3 * D), F32),
        compiler_params=_params("arbitrary", "arbitrary"),
    )(cvec, ada_w, ada_b.reshape(depth, 1, 3 * D))


PROJ_COLS = 512


def _rope_table(ctx_len, seq_len):
    n_freq = RET_DK // 4
    freqs = ROPE_BASE ** (-jnp.arange(n_freq, dtype=F32) / n_freq)
    t = jnp.arange(seq_len, dtype=jnp.int32)
    row, col = (t // GRID_W).astype(F32), (t % GRID_W).astype(F32)
    ang = jnp.concatenate([row[:, None] * freqs, col[:, None] * freqs], axis=-1)
    ang = jnp.concatenate([jnp.zeros((ctx_len, RET_DK // 2), F32), ang], axis=0)
    return jnp.concatenate([jnp.cos(ang), jnp.sin(ang)], axis=-1)


RET_ROW_LANES = 128


def _ret_table_kernel(dl_ref, mask_ref, row_ref, cd_ref):
    c = ROW_TILE
    i = lax.broadcasted_iota(jnp.int32, (c, c), 0).astype(F32)
    j = lax.broadcasted_iota(jnp.int32, (c, c), 1).astype(F32)
    ir = i[:, :RET_ROW_LANES]

    def log_sigmoid(x):
        return jnp.minimum(x, 0.0) - jnp.log(1.0 + jnp.exp(-jnp.abs(x)))

    for h in range(RET_HEADS):
        lg_f = log_sigmoid(jnp.full((c, c), dl_ref[0, h], F32))
        lg_b = log_sigmoid(jnp.full((c, c), dl_ref[1, h], F32))
        fwd = jnp.where(i >= j, jnp.exp(jnp.maximum(i - j, 0.0) * lg_f), 0.0)
        bwd = jnp.where(j >= i, jnp.exp(jnp.maximum(j - i, 0.0) * lg_b), 0.0)
        mask_ref[h] = fwd + bwd
        lf = log_sigmoid(jnp.full((c, RET_ROW_LANES), dl_ref[0, h], F32))
        lb = log_sigmoid(jnp.full((c, RET_ROW_LANES), dl_ref[1, h], F32))
        row_ref[h, 0] = jnp.exp((ir + 1.0) * lf)
        row_ref[h, 1] = jnp.exp((c - 1.0 - ir) * lf)
        row_ref[h, 2] = jnp.exp((c - ir) * lb)
        row_ref[h, 3] = jnp.exp(ir * lb)
        cd_ref[h, 0] = jnp.exp(c * log_sigmoid(jnp.full((8, RET_DV), dl_ref[0, h], F32)))
        cd_ref[h, 1] = jnp.exp(c * log_sigmoid(jnp.full((8, RET_DV), dl_ref[1, h], F32)))


def _ret_tables(decay_logit):
    c = ROW_TILE
    return pl.pallas_call(
        _ret_table_kernel,
        in_specs=[pl.BlockSpec(memory_space=pltpu.SMEM)],
        out_shape=(jax.ShapeDtypeStruct((RET_HEADS, c, c), F32),
                   jax.ShapeDtypeStruct((RET_HEADS, 4, c, RET_ROW_LANES), F32),
                   jax.ShapeDtypeStruct((RET_HEADS, 2, 8, RET_DV), F32)),
    )(decay_logit)


def _lanes(t, width):
    return jnp.concatenate([t] * (width // RET_ROW_LANES), axis=1)


def _ret_heads(q_ref, k_ref, v_ref, k_col=0, v_col=0):
    return [(q_ref[:, h * RET_DK:(h + 1) * RET_DK], k_ref[:, k_col + h * RET_DK:k_col + (h + 1) * RET_DK],
             v_ref[:, v_col + h * RET_DV:v_col + (h + 1) * RET_DV]) for h in range(RET_HEADS)]


def _ret_fwd_kernel(x_ref, mod_ref, g_ref, w_ref, cs_ref, mask_ref, row_ref, cd_ref, p_ref, o_ref, s_ref):
    @pl.when(pl.program_id(1) == 0)
    def _():
        s_ref[...] = jnp.zeros_like(s_ref)

    h_in = _norm_mod(x_ref[...], g_ref[...], mod_ref[...]).astype(BF16)
    half = RET_DK // 2
    k_col = RET_HEADS * RET_DK
    v_col = 2 * k_col
    z_col = v_col + RET_HEADS * RET_DV
    cos = cs_ref[:, :half]
    sin = cs_ref[:, half:]
    for c0 in range(0, v_col, RET_DK):
        acc = jnp.dot(h_in, w_ref[:, c0:c0 + RET_DK], preferred_element_type=F32)
        t1, t2 = acc[:, :half], acc[:, half:]
        p_ref[:, c0:c0 + half] = (t1 * cos - t2 * sin).astype(BF16)
        p_ref[:, c0 + half:c0 + RET_DK] = (t1 * sin + t2 * cos).astype(BF16)
    for c0 in range(v_col, z_col, PROJ_COLS):
        p_ref[:, c0:c0 + PROJ_COLS] = jnp.dot(h_in, w_ref[:, c0:c0 + PROJ_COLS],
                                              preferred_element_type=F32).astype(BF16)

    def gate_cols(h):
        c0 = z_col + h * RET_DV
        acc = jnp.dot(h_in, w_ref[:, c0:c0 + RET_DV], preferred_element_type=F32)
        p_ref[:, c0:c0 + RET_DV] = _silu(acc).astype(BF16)

    qkv = _ret_heads(p_ref, p_ref, p_ref, k_col, v_col)
    sc = {}

    def scores(h):
        q, k, _ = qkv[h]
        sc[h] = lax.dot_general(q, k, _NT, preferred_element_type=F32)

    def finish(h):
        q, k, v = qkv[h]
        s = s_ref[h]
        qs = _bdot(q, s)
        kd = k.astype(F32) * _lanes(row_ref[h, 1], RET_DK)
        s_ref[h] = s * cd_ref[h, 0, 0:1, :] + _bdot(kd, v, _TN)
        o = _bdot(sc[h] * mask_ref[h], v)
        o_ref[:, h * RET_DV:(h + 1) * RET_DV] = (o + qs * _lanes(row_ref[h, 0], RET_DV)).astype(BF16)

    scores(0)
    for h in range(RET_HEADS):
        if h + 1 < RET_HEADS:
            scores(h + 1)
        gate_cols(h)
        finish(h)


def _ret_bwd_kernel(q_ref, k_ref, v_ref, z_ref, op_ref, x_ref, mod_ref, wo_ref, row_ref, cd_ref, fg_ref,
                    xo_ref, s_ref, *, final):
    @pl.when(pl.program_id(1) == 0)
    def _():
        s_ref[...] = jnp.zeros_like(s_ref)

    qkv = _ret_heads(q_ref, k_ref, v_ref)
    gz = {}

    def inter(h):
        q, k, v = qkv[h]
        vv = slice(h * RET_DV, (h + 1) * RET_DV)
        s = s_ref[h]
        o = op_ref[:, vv].astype(F32) + _bdot(q, s) * _lanes(row_ref[h, 2], RET_DV)
        kd = k.astype(F32) * _lanes(row_ref[h, 3], RET_DK)
        s_ref[h] = s * cd_ref[h, 1, 0:1, :] + _bdot(kd, v, _TN)
        o = o * lax.rsqrt(jnp.mean(o * o, axis=-1, keepdims=True) + NORM_EPS)
        gz[h] = (o * z_ref[:, vv].astype(F32)).astype(BF16)

    acc = jnp.zeros((ROW_TILE, D), F32)
    inter(0)
    for h in range(RET_HEADS):
        if h + 1 < RET_HEADS:
            inter(h + 1)
        acc = acc + jnp.dot(gz[h], wo_ref[h * RET_DV:(h + 1) * RET_DV, :], preferred_element_type=F32)
    xn = x_ref[...] + mod_ref[:, 2 * D:] * acc
    if final:
        xn = xn * lax.rsqrt(jnp.mean(xn * xn, axis=-1, keepdims=True) + NORM_EPS) * fg_ref[...]
    xo_ref[...] = xn


def _bwd_tile(tiles):
    return lambda s: jnp.where(s == 0, 0, tiles - s)


def _retention(xc, mod4, g, w_in, rope, w_out, tables, layer, final_g):
    n_batch, lc, _ = xc.shape
    tiles = lc // ROW_TILE
    width = RET_HEADS * RET_DV
    state = pltpu.VMEM((RET_HEADS, RET_DK, RET_DV), F32)
    mask, row, cd = tables
    mask_spec = pl.BlockSpec(mask.shape, lambda b, s: (0, 0, 0))
    row_spec = pl.BlockSpec(row.shape, lambda b, s: (0, 0, 0, 0))
    cd_spec = pl.BlockSpec(cd.shape, lambda b, s: (0, 0, 0, 0))

    def col_spec(width_, col, tile_of_step):
        return pl.BlockSpec((None, ROW_TILE, width_), lambda b, s: (b, tile_of_step(s), col))

    n_proj = w_in.shape[1]
    proj, o_part = pl.pallas_call(
        _ret_fwd_kernel,
        grid=(n_batch, tiles),
        in_specs=[col_spec(D, 0, _identity), _mod_spec(layer, n_batch, _identity),
                  pl.BlockSpec((1, D), lambda b, s: (0, 0)), pl.BlockSpec((D, n_proj), lambda b, s: (0, 0)),
                  pl.BlockSpec((ROW_TILE, RET_DK), lambda b, s: (s, 0)), mask_spec, row_spec, cd_spec],
        out_specs=[col_spec(n_proj, 0, _identity), col_spec(width, 0, _identity)],
        out_shape=[jax.ShapeDtypeStruct((n_batch, lc, n_proj), BF16),
                   jax.ShapeDtypeStruct((n_batch, lc, width), BF16)],
        scratch_shapes=[state],
        compiler_params=_params("parallel", "arbitrary"),
    )(xc, mod4, g.reshape(1, D), w_in, rope, mask, row, cd)

    bt = _bwd_tile(tiles)
    final = final_g is not None
    if final:
        ctx_tiles = 1
        out_spec = col_spec(D, 0, lambda s: bt(jnp.maximum(s, 1)) - ctx_tiles)
        out_shape = jax.ShapeDtypeStruct((n_batch, lc - ctx_tiles * ROW_TILE, D), F32)
        fg = final_g.reshape(1, D)
    else:
        out_spec = col_spec(D, 0, bt)
        out_shape = jax.ShapeDtypeStruct(xc.shape, F32)
        fg = jnp.ones((1, D), F32)
    return pl.pallas_call(
        functools.partial(_ret_bwd_kernel, final=final),
        grid=(n_batch, tiles),
        in_specs=[col_spec(RET_HEADS * RET_DK, 0, bt), col_spec(RET_HEADS * RET_DK, 1, bt),
                  col_spec(width, 1, bt), col_spec(width, 2, bt), col_spec(width, 0, bt),
                  col_spec(D, 0, bt), _mod_spec(layer, n_batch, bt),
                  pl.BlockSpec((width, D), lambda b, s: (0, 0)), row_spec, cd_spec,
                  pl.BlockSpec((1, D), lambda b, s: (0, 0))],
        out_specs=out_spec,
        out_shape=out_shape,
        scratch_shapes=[state],
        compiler_params=_params("parallel", "arbitrary"),
    )(proj, proj, proj, proj, o_part, xc, mod4, w_out, row, cd, fg)


def _gm_kernel(x_ref, mod_ref, g_ref, w_ref, vg_ref, ws_ref, bs_ref, wo_ref, xo_ref, v_ref):
    mod = mod_ref[...]
    h = _norm_mod(x_ref[...], g_ref[...], mod).astype(BF16)
    for c0 in range(0, GM_WIDTH, PROJ_COLS):
        v_ref[:, c0:c0 + PROJ_COLS] = jnp.dot(h, w_ref[:, GM_WIDTH + c0:GM_WIDTH + c0 + PROJ_COLS],
                                              preferred_element_type=F32)
    v = v_ref[...]
    v = v - jnp.mean(v, axis=-1, keepdims=True)
    vn = (v * lax.rsqrt(jnp.mean(v * v, axis=-1, keepdims=True) + NORM_EPS) * vg_ref[...]).astype(BF16)

    c = GM_CHUNK
    gw = GM_WIDTH // GM_GROUPS
    gated = {}

    def group(g):
        cols = slice(g * gw, (g + 1) * gw)
        u = jnp.dot(h, w_ref[:, cols], preferred_element_type=F32)
        z = jnp.dot(h, w_ref[:, 2 * GM_WIDTH + g * gw:2 * GM_WIDTH + (g + 1) * gw], preferred_element_type=F32)
        mixed = jnp.concatenate([jnp.dot(ws_ref[g], vn[ci * c:(ci + 1) * c, cols], preferred_element_type=F32)
                                 for ci in range(ROW_TILE // c)], axis=0)
        mixed = mixed + jnp.concatenate([bs_ref[:, g:g + 1]] * (ROW_TILE // c), axis=0)
        gated[g] = (u * mixed * _silu(z)).astype(BF16)

    acc = jnp.zeros((ROW_TILE, D), F32)
    group(0)
    for g in range(GM_GROUPS):
        if g + 1 < GM_GROUPS:
            group(g + 1)
        acc = acc + jnp.dot(gated[g], wo_ref[g * gw:(g + 1) * gw, :], preferred_element_type=F32)
    xo_ref[...] = x_ref[...] + mod[:, 2 * D:] * acc


def _gmlp(xc, mod4, g, w_in, vnorm_g, w_s, b_s, w_out, layer):
    n_batch, lc, _ = xc.shape
    tiles = lc // ROW_TILE
    tile_spec = pl.BlockSpec((None, ROW_TILE, D), lambda b, t: (b, t, 0))

    def full(a):
        return pl.BlockSpec(a.shape, lambda b, t: (0,) * a.ndim)

    return pl.pallas_call(
        _gm_kernel,
        grid=(n_batch, tiles),
        in_specs=[tile_spec, _mod_spec(layer, n_batch, _identity), full(g), full(w_in), full(vnorm_g), full(w_s),
                  full(b_s), full(w_out)],
        out_specs=tile_spec,
        out_shape=jax.ShapeDtypeStruct(xc.shape, F32),
        scratch_shapes=[pltpu.VMEM((ROW_TILE, GM_WIDTH), F32)],
        compiler_params=_params("parallel", "parallel"),
    )(xc, mod4, g, w_in, vnorm_g, w_s, b_s, w_out)


def _head_sum(x, e_ref):
    outs = []
    e = e_ref[...]
    for g in range(x.shape[-1] // WKV_LANES):
        xs = x[:, g * WKV_LANES:(g + 1) * WKV_LANES]
        hi = xs.astype(BF16)
        lo = (xs - hi.astype(F32)).astype(BF16)
        outs.append(jnp.dot(hi, e, preferred_element_type=F32) + jnp.dot(lo, e, preferred_element_type=F32))
    return jnp.concatenate(outs, axis=1)


def _rw_prep_kernel(xp_ref, x_ref, xn_ref, mod_ref, g_ref, mu_ref, wm_ref, w1_ref, w2_ref, a1_ref, a2_ref,
                    vec_ref, e_ref, r_ref, v_ref, kk_ref, k_ref, b_ref, ld_ref, bonus_ref, z_ref,
                    *, grid_rows):
    t = pl.program_id(1)
    is_ctx = t == 0
    g, mod = g_ref[...], mod_ref[...]
    h = _norm_mod(x_ref[...], g, mod)
    h_up = _norm_mod(xp_ref[...], g, mod)
    h_dn = _norm_mod(xn_ref[...], g, mod)
    n = ROW_TILE
    prev = pltpu.roll(h, 1, 0)
    nxt = pltpu.roll(h, n - 1, 0)
    ext = jnp.concatenate([h_up, h, h_dn], axis=0)
    up, down = ext[:n], ext[2 * GRID_W:]

    lane_q = lax.broadcasted_iota(jnp.int32, (n, D), 1) // (D // 4)
    i = lax.broadcasted_iota(jnp.int32, (n, D), 0)
    grow = (t - 1) * (n // GRID_W) + i // GRID_W
    pos = jnp.where(is_ctx, i, i % GRID_W)
    last_pos = jnp.where(is_ctx, n - 1, GRID_W - 1)
    prev_lanes = jnp.where(is_ctx, 2, 1)
    next_lanes = jnp.where(is_ctx, 4, 2)
    prev = jnp.where(pos > 0, prev, 0.0)
    nxt = jnp.where(pos < last_pos, nxt, 0.0)
    up = jnp.where(grow > 0, up, 0.0)
    down = jnp.where(grow < grid_rows - 1, down, 0.0)
    shifted = jnp.where(lane_q < prev_lanes, prev,
                        jnp.where(lane_q < next_lanes, nxt, jnp.where(lane_q == 2, up, down)))
    xx = shifted - h

    hb, xb = h.astype(BF16), xx.astype(BF16)

    def mix(p):
        return hb + xb * mu_ref[p:p + 1, :].astype(BF16)

    vec = vec_ref[...]
    r = jnp.dot(mix(0), wm_ref[0], preferred_element_type=F32)
    k = jnp.dot(mix(2), wm_ref[1], preferred_element_type=F32)
    v = jnp.dot(mix(3), wm_ref[2], preferred_element_type=F32)
    z_ref[...] = jnp.dot(mix(5), wm_ref[3], preferred_element_type=F32).astype(BF16)
    tw = jnp.tanh(jnp.dot(mix(1), w1_ref[...], preferred_element_type=F32))
    ta = jnp.dot(mix(4), a1_ref[...], preferred_element_type=F32)

    kk = k * vec[4:5]
    kk = kk / jnp.maximum(jnp.sqrt(_head_sum(kk * kk, e_ref)), 1e-12)
    r_ref[...] = r
    v_ref[...] = v
    kk_ref[...] = kk
    bonus = jnp.zeros_like(r)
    for d in range(2):
        lw = vec[d:d + 1] + _bdot(tw, w2_ref[d])
        ld_ref[d] = -math.exp(-0.5) / (1.0 + jnp.exp(-lw))
        a = 1.0 / (1.0 + jnp.exp(-(vec[2 + d:3 + d] + _bdot(ta, a2_ref[d]))))
        kd = k * (1.0 + (a - 1.0) * vec[5:6])
        k_ref[d] = kd
        b_ref[d] = kk * a
        bonus = bonus + r * kd * vec[6:7]
    bonus_ref[...] = (_head_sum(bonus, e_ref) * v).astype(BF16)


def _rw_prepare(xc, mod4, g, mu, wm, w1, w2, a1, a2, vec, e, layer, grid_rows):
    n_batch, lc, _ = xc.shape
    tiles = lc // ROW_TILE
    per_tile = ROW_TILE // GRID_W
    last = lc // GRID_W - 1

    def full(a):
        return pl.BlockSpec(a.shape, lambda b, t: (0,) * a.ndim)

    tile_spec = pl.BlockSpec((None, ROW_TILE, D), lambda b, t: (b, t, 0))
    dir_spec = pl.BlockSpec((2, None, ROW_TILE, D), lambda b, t: (0, b, t, 0))
    one = jax.ShapeDtypeStruct((n_batch, lc, D), F32)
    two = jax.ShapeDtypeStruct((2, n_batch, lc, D), F32)
    half = jax.ShapeDtypeStruct((n_batch, lc, D), BF16)
    return pl.pallas_call(
        functools.partial(_rw_prep_kernel, grid_rows=grid_rows),
        grid=(n_batch, tiles),
        in_specs=[pl.BlockSpec((None, GRID_W, D), lambda b, t: (b, jnp.maximum(t * per_tile - 1, 0), 0)),
                  tile_spec,
                  pl.BlockSpec((None, GRID_W, D), lambda b, t: (b, jnp.minimum((t + 1) * per_tile, last), 0)),
                  _mod_spec(layer, n_batch, _identity), full(g), full(mu), full(wm), full(w1), full(w2),
                  full(a1), full(a2), full(vec), full(e)],
        out_specs=[tile_spec, tile_spec, tile_spec, dir_spec, dir_spec, dir_spec, tile_spec, tile_spec],
        out_shape=[one, one, one, two, two, two, half, half],
        compiler_params=_params("parallel", "parallel"),
    )(xc, xc, xc, mod4, g, mu, wm, w1, w2, a1, a2, vec, e)


def _wkv_masks(reverse):
    c, hg = WKV_CHUNK, WKV_GROUP
    t = lax.broadcasted_iota(jnp.int32, (c, hg * c), 0)
    s = lax.broadcasted_iota(jnp.int32, (c, hg * c), 1) % c
    strict = jnp.where((s > t) if reverse else (s < t), 1.0, 0.0)
    incl = jnp.where((s >= t) if reverse else (s <= t), 1.0, 0.0)
    eye = jnp.where(s == t, 1.0, 0.0)
    return strict, jnp.concatenate([incl, incl], axis=1), eye, jnp.concatenate([incl[:, :c]] * 3, axis=1).astype(BF16)


def _wkv_stages(chains, m4, bd, bd_state):
    c, w, hg = WKV_CHUNK, WKV_LANES, WKV_GROUP
    n = range(len(chains))
    refs, rows, masks, reverse = zip(*[(ch["refs"], (ch["rows"], ch["lanes"]), ch["masks"], ch["reverse"])
                                       for ch in chains])
    strict, incl2, eye, cum_mat = zip(*masks)

    def dot(a, b, dims=None):
        if dims is None:
            return jnp.dot(a, b, preferred_element_type=F32)
        return lax.dot_general(a, b, dims, preferred_element_type=F32)

    def stack(x):
        return jnp.where(jnp.concatenate([m4] * (x.shape[1] // w), axis=1), jnp.concatenate([x] * hg, axis=0), 0)

    def blockdiag(x):
        return jnp.where(bd, jnp.concatenate([x.astype(BF16)] * hg, axis=0), 0)

    ld = [refs[i][5][rows[i]] for i in n]
    cum = [dot(cum_mat[i], jnp.concatenate(_split3(ld[i]), axis=0)) for i in n]
    yield
    tot = [cum[i][0:1] if reverse[i] else cum[i][c - 1:c] for i in n]
    e_out = [jnp.exp(-cum[i]) for i in n]
    vb = [refs[i][4][rows[i]].astype(BF16) for i in n]
    rt = [(refs[i][0][rows[i]] * jnp.exp(cum[i])).astype(BF16) for i in n]
    at = [(-refs[i][1][rows[i]] * jnp.exp(cum[i] - ld[i])).astype(BF16) for i in n]
    kt = [(refs[i][3][rows[i]] * e_out[i]).astype(BF16) for i in n]
    bt = [(refs[i][2][rows[i]] * e_out[i]).astype(BF16) for i in n]

    sc = [dot(jnp.concatenate([at[i], rt[i]], axis=0), jnp.concatenate([stack(kt[i]), stack(bt[i])], axis=0), _NT)
          for i in n]
    yield
    a_ak = [(sc[i][:c, :w] * strict[i]).astype(BF16) for i in n]
    p = [sc[i][:c, w:] * strict[i] for i in n]
    a_r = [(sc[i][c:] * incl2[i]).astype(BF16) for i in n]
    tinv = [eye[i] + p[i] for i in n]
    for _ in range(c.bit_length() - 2):
        p = [dot(p[i].astype(BF16), blockdiag(p[i])) for i in n]
        yield
        tinv = [tinv[i] + dot(p[i].astype(BF16), blockdiag(tinv[i])) for i in n]
        yield
    sv = [stack(vb[i]) for i in n]
    akv = [dot(a_ak[i], sv[i]).astype(BF16) for i in n]
    yield
    tx = [dot(tinv[i].astype(BF16), stack(jnp.concatenate([at[i], akv[i]], axis=1))) for i in n]
    yield

    ht = [ch["ht"][ch["slot"]] for ch in chains]
    wrh = [dot(jnp.concatenate([tx[i][:, :w].astype(BF16), rt[i]], axis=0), ht[i].astype(BF16), _NT) for i in n]
    yield
    ub = [(wrh[i][:c] + tx[i][:, w:]).astype(BF16) for i in n]
    y = [wrh[i][c:] + dot(a_r[i], jnp.concatenate([sv[i], stack(ub[i])], axis=0)) for i in n]
    yield
    e_end = [jnp.exp(tot[i] - cum[i]) for i in n]
    kb = [jnp.concatenate([(refs[i][3][rows[i]] * e_end[i]).astype(BF16),
                           (refs[i][2][rows[i]] * e_end[i]).astype(BF16)], axis=0) for i in n]
    upd = [dot(jnp.concatenate([vb[i], ub[i]], axis=0), kb[i], _TN) for i in n]
    for i, ch in enumerate(chains):
        ch["y"][rows[i]] = y[i].astype(BF16)
        ch["ht"][ch["slot"]] = jnp.where(bd_state, ht[i] * jnp.exp(tot[i]) + upd[i], 0.0)


def _wkv_kernel(rf, kkf, bf, kf, vf, ldf, rb, kkb, bb, kb, vb, ldb, yf_ref, yb_ref, ht_ref):
    @pl.when(pl.program_id(1) == 0)
    def _():
        ht_ref[...] = jnp.zeros_like(ht_ref)

    c, w, hg = WKV_CHUNK, WKV_LANES, WKV_GROUP
    n_chunks = ROW_TILE // c
    lane_head = lax.broadcasted_iota(jnp.int32, (hg * c, w), 1) // RW_HEAD
    row_head = lax.broadcasted_iota(jnp.int32, (hg * c, w), 0) // c
    m4 = lane_head == row_head
    bd = lax.broadcasted_iota(jnp.int32, (hg * c, hg * c), 1) // c == row_head
    bd_state = (lax.broadcasted_iota(jnp.int32, (w, w), 0) // RW_HEAD
                == lax.broadcasted_iota(jnp.int32, (w, w), 1) // RW_HEAD)
    masks_f, masks_b = _wkv_masks(False), _wkv_masks(True)
    groups = D // w

    def chunk(step):
        rows_f = slice(step * c, (step + 1) * c)
        rows_b = slice((n_chunks - 1 - step) * c, (n_chunks - step) * c)
        chains = []
        for g in range(groups):
            lanes = slice(g * w, (g + 1) * w)
            chains.append(dict(refs=(rf, kkf, bf, kf, vf, ldf), rows=rows_f, lanes=lanes, masks=masks_f,
                               reverse=False, ht=ht_ref, slot=g, y=yf_ref))
            chains.append(dict(refs=(rb, kkb, bb, kb, vb, ldb), rows=rows_b, lanes=lanes, masks=masks_b,
                               reverse=True, ht=ht_ref, slot=groups + g, y=yb_ref))
        return _wkv_stages(chains, m4, bd, bd_state)

    chunks = [chunk(step) for step in range(n_chunks)]
    for _ in range(WKV_SETUP_STAGES):
        next(chunks[0])
    for step in range(n_chunks):
        for k in range(WKV_INVERSE_STAGES + WKV_APPLY_STAGES):
            next(chunks[step], None)
            if step + 1 < n_chunks and k in WKV_SETUP_SLOTS:
                next(chunks[step + 1])


def _wkv(r, kk, b, k, v, ld):
    n_batch, lc, _ = r.shape
    tiles = lc // ROW_TILE
    bt = _bwd_tile(tiles)

    def specs(tile_of, direction):
        one = pl.BlockSpec((None, ROW_TILE, D), lambda bb, s: (bb, tile_of(s), 0))
        two = pl.BlockSpec((None, None, ROW_TILE, D), lambda bb, s: (direction, bb, tile_of(s), 0))
        return one, two

    f1, f2 = specs(_identity, 0)
    b1, b2 = specs(bt, 1)
    return pl.pallas_call(
        _wkv_kernel,
        grid=(n_batch, tiles),
        in_specs=[f1, f1, f2, f2, f1, f2, b1, b1, b2, b2, b1, b2],
        out_specs=[f1, b1],
        out_shape=[jax.ShapeDtypeStruct(r.shape, BF16)] * 2,
        scratch_shapes=[pltpu.VMEM((2 * (D // WKV_LANES), WKV_LANES, WKV_LANES), F32)],
        compiler_params=_params("parallel", "arbitrary"),
    )(r, kk, b, k, v, ld, r, kk, b, k, v, ld)


def _rw_out_kernel(yf_ref, yb_ref, bonus_ref, z_ref, x_ref, mod_ref, vec_ref, e_ref, wo_ref, xo_ref):
    y = yf_ref[...].astype(F32) + yb_ref[...].astype(F32)
    inv = 1.0 / RW_HEAD
    yc = y - _head_sum(y, e_ref) * inv
    yn = yc * lax.rsqrt(_head_sum(yc * yc, e_ref) * inv + RW_LNX_EPS)
    yn = yn * vec_ref[0:1] + vec_ref[1:2]
    o = (yn + bonus_ref[...].astype(F32)) * _silu(z_ref[...].astype(F32))
    xo_ref[...] = x_ref[...] + mod_ref[:, 2 * D:] * _bdot(o, wo_ref[...])


def _rw_output(xc, y_f, y_b, bonus, z, mod4, vec, e, w_out, layer):
    n_batch, lc, _ = xc.shape
    tiles = lc // ROW_TILE
    tile_spec = pl.BlockSpec((None, ROW_TILE, D), lambda b, t: (b, t, 0))

    def full(a):
        return pl.BlockSpec(a.shape, lambda b, t: (0,) * a.ndim)

    return pl.pallas_call(
        _rw_out_kernel,
        grid=(n_batch, tiles),
        in_specs=[tile_spec, tile_spec, tile_spec, tile_spec, tile_spec, _mod_spec(layer, n_batch, _identity),
                  full(vec), full(e), full(w_out)],
        out_specs=tile_spec,
        out_shape=jax.ShapeDtypeStruct(xc.shape, F32),
        compiler_params=_params("parallel", "parallel"),
    )(y_f, y_b, bonus, z, xc, mod4, vec, e, w_out)


def _rwkv(xc, mod4, g, mu, w_rkvg, w0, w1, w2, a0, a1, a2, k_k, k_a, r_k, lnx_g, lnx_b, w_out, layer, grid_rows):
    zeros = jnp.zeros((RW_LORA, D), F32)
    w1c = jnp.concatenate([w1[0], w1[1]], axis=1).astype(BF16)
    a1c = jnp.concatenate([a1[0], a1[1]], axis=1).astype(BF16)
    w2p = jnp.stack([jnp.concatenate([w2[0], zeros]), jnp.concatenate([zeros, w2[1]])]).astype(BF16)
    a2p = jnp.stack([jnp.concatenate([a2[0], zeros]), jnp.concatenate([zeros, a2[1]])]).astype(BF16)
    zrow = jnp.zeros((D,), F32)
    vec = jnp.stack([w0[0], w0[1], a0[0], a0[1], k_k, k_a, r_k.reshape(D), zrow])
    lane = jnp.arange(WKV_LANES) // RW_HEAD
    e = (lane[:, None] == lane[None, :]).astype(BF16)
    r, v, kk, k, b, ld, bonus, z = _rw_prepare(xc, mod4, g.reshape(1, D), mu, w_rkvg.astype(BF16), w1c, w2p,
                                               a1c, a2p, vec, e, layer, grid_rows)
    y_f, y_b = _wkv(r, kk, b, k, v, ld)
    vec_o = jnp.stack([lnx_g, lnx_b] + [zrow] * 6)
    return _rw_output(xc, y_f, y_b, bonus, z, mod4, vec_o, e, w_out.astype(BF16), layer)


def kernel(x, c, ctx, c_ctx, ada_w, ada_b, norm_g, final_g, ret_w_in, ret_decay, ret_w_out, gm_w_in, gm_vnorm_g,
           gm_w_s, gm_b_s, gm_w_out, rw_mu, rw_w_rkvg, rw_w0, rw_w1, rw_w2, rw_a0, rw_a1, rw_a2, rw_k_k, rw_k_a,
           rw_r_k, rw_lnx_g, rw_lnx_b, rw_w_out):
    n_batch, seq_len, _ = x.shape
    ctx_len = ctx.shape[1]
    depth = ada_w.shape[0]
    assert ctx_len == ROW_TILE and seq_len % ROW_TILE == 0 and n_batch < 8

    xc = jnp.concatenate([ctx, x], axis=1)
    cvec = jnp.zeros((8, D), F32).at[:n_batch].set(c).at[n_batch].set(c_ctx)
    mod4 = _modulation(cvec, ada_w, ada_b).reshape(depth, 8, 1, 3 * D)
    rope = _rope_table(ctx_len, seq_len)
    k_scale = jnp.concatenate([jnp.ones((RET_HEADS * RET_DK,), F32),
                               jnp.full((RET_HEADS * RET_DK,), RET_DK ** -0.5, F32),
                               jnp.ones((2 * RET_HEADS * RET_DV,), F32)])

    for i in range(depth):
        kind, j = i % N_MIXERS, i // N_MIXERS
        if kind == 0:
            w_in = (ret_w_in[j] * k_scale).astype(BF16)
            xc = _retention(xc, mod4, norm_g[i], w_in, rope, ret_w_out[j].astype(BF16), _ret_tables(ret_decay[j]), i,
                            final_g if i == depth - 1 else None)
        elif kind == 1:
            xc = _gmlp(xc, mod4, norm_g[i].reshape(1, D), gm_w_in[j].astype(BF16), gm_vnorm_g[j].reshape(1, GM_WIDTH),
                       gm_w_s[j].astype(BF16), gm_b_s[j].T, gm_w_out[j].astype(BF16), i)
        else:
            xc = _rwkv(xc, mod4, norm_g[i], rw_mu[j], rw_w_rkvg[j], rw_w0[j], rw_w1[j], rw_w2[j], rw_a0[j],
                       rw_a1[j], rw_a2[j], rw_k_k[j], rw_k_a[j], rw_r_k[j], rw_lnx_g[j], rw_lnx_b[j],
                       rw_w_out[j], i, seq_len // GRID_W)
    assert (depth - 1) % N_MIXERS == 0, "the final norm is fused into a retention layer"
    return xc
```

```python
import functools
import math

import jax
import jax.numpy as jnp
from jax import lax
from jax.experimental import pallas as pl
from jax.experimental.pallas import tpu as pltpu

F32 = jnp.float32
BF16 = jnp.bfloat16

D = 1024
GRID_W = 64
NORM_EPS = 1e-6
N_MIXERS = 3
ROW_TILE = 256
RET_HEADS = 4
RET_DK = 256
RET_DV = 512
ROPE_BASE = 10000.0
GM_WIDTH = 2 * D
GM_GROUPS = 8
GM_CHUNK = 128
RW_HEAD = 64
RW_LORA = 64
RW_LNX_EPS = 64e-5
WKV_CHUNK = 64
WKV_LANES = 256
WKV_GROUP = WKV_LANES // RW_HEAD
WKV_SETUP_STAGES, WKV_INVERSE_STAGES, WKV_APPLY_STAGES = 2, 10, 5
WKV_SETUP_SLOTS = (10, 12)
VMEM_LIMIT = 56 * 1024 * 1024

_NT = (((1,), (1,)), ((), ()))
_TN = (((0,), (0,)), ((), ()))


def _params(*sem):
    return pltpu.CompilerParams(dimension_semantics=sem, vmem_limit_bytes=VMEM_LIMIT)


def _silu(x):
    return x / (1.0 + jnp.exp(-x))


def _bdot(a, b, dims=None):
    a = a.astype(BF16)
    b = b.astype(BF16)
    if dims is None:
        return jnp.dot(a, b, preferred_element_type=F32)
    return lax.dot_general(a, b, dims, preferred_element_type=F32)


def _split3(x):
    hi = x.astype(BF16)
    r1 = x - hi.astype(F32)
    mid = r1.astype(BF16)
    lo = (r1 - mid.astype(F32)).astype(BF16)
    return hi, mid, lo


def _norm_mod(x, g, mod):
    ms = jnp.mean(x * x, axis=-1, keepdims=True)
    y = x * lax.rsqrt(ms + NORM_EPS) * g
    return y * (1.0 + mod[:, D:2 * D]) + mod[:, :D]


def _mod_spec(layer, n_batch, tile_of_step):
    def index(b, s):
        t = tile_of_step(s)
        return (layer, jnp.where(t == 0, n_batch, b), 0, 0)
    return pl.BlockSpec((None, None, 1, 3 * D), index)


def _identity(s):
    return s


def _mod_kernel(c_ref, w_ref, b_ref, o_ref):
    s = _silu(c_ref[...])
    o_ref[...] = jnp.dot(s, w_ref[...], preferred_element_type=F32,
                         precision=lax.Precision.HIGHEST) + b_ref[...]


def _modulation(cvec, ada_w, ada_b):
    depth = ada_w.shape[0]
    return pl.pallas_call(
        _mod_kernel,
        grid=(depth, 3),
        in_specs=[pl.BlockSpec((8, D), lambda i, j: (0, 0)),
                  pl.BlockSpec((None, D, D), lambda i, j: (i, 0, j)),
                  pl.BlockSpec((None, 1, D), lambda i, j: (i, 0, j))],
        out_specs=pl.BlockSpec((None, 8, D), lambda i, j: (i, 0, j)),
        out_shape=jax.ShapeDtypeStruct((depth, 8, 3 * D), F32),
        compiler_params=_params("arbitrary", "arbitrary"),
    )(cvec, ada_w, ada_b.reshape(depth, 1, 3 * D))


PROJ_COLS = 512


def _rope_table(ctx_len, seq_len):
    n_freq = RET_DK // 4
    freqs = ROPE_BASE ** (-jnp.arange(n_freq, dtype=F32) / n_freq)
    t = jnp.arange(seq_len, dtype=jnp.int32)
    row, col = (t // GRID_W).astype(F32), (t % GRID_W).astype(F32)
    ang = jnp.concatenate([row[:, None] * freqs, col[:, None] * freqs], axis=-1)
    ang = jnp.concatenate([jnp.zeros((ctx_len, RET_DK // 2), F32), ang], axis=0)
    return jnp.concatenate([jnp.cos(ang), jnp.sin(ang)], axis=-1)


RET_ROW_LANES = 128


def _ret_table_kernel(dl_ref, mask_ref, row_ref, cd_ref):
    c = ROW_TILE
    i = lax.broadcasted_iota(jnp.int32, (c, c), 0).astype(F32)
    j = lax.broadcasted_iota(jnp.int32, (c, c), 1).astype(F32)
    ir = i[:, :RET_ROW_LANES]

    def log_sigmoid(x):
        return jnp.minimum(x, 0.0) - jnp.log(1.0 + jnp.exp(-jnp.abs(x)))

    for h in range(RET_HEADS):
        lg_f = log_sigmoid(jnp.full((c, c), dl_ref[0, h], F32))
        lg_b = log_sigmoid(jnp.full((c, c), dl_ref[1, h], F32))
        fwd = jnp.where(i >= j, jnp.exp(jnp.maximum(i - j, 0.0) * lg_f), 0.0)
        bwd = jnp.where(j >= i, jnp.exp(jnp.maximum(j - i, 0.0) * lg_b), 0.0)
        mask_ref[h] = fwd + bwd
        lf = log_sigmoid(jnp.full((c, RET_ROW_LANES), dl_ref[0, h], F32))
        lb = log_sigmoid(jnp.full((c, RET_ROW_LANES), dl_ref[1, h], F32))
        row_ref[h, 0] = jnp.exp((ir + 1.0) * lf)
        row_ref[h, 1] = jnp.exp((c - 1.0 - ir) * lf)
        row_ref[h, 2] = jnp.exp((c - ir) * lb)
        row_ref[h, 3] = jnp.exp(ir * lb)
        cd_ref[h, 0] = jnp.exp(c * log_sigmoid(jnp.full((8, RET_DV), dl_ref[0, h], F32)))
        cd_ref[h, 1] = jnp.exp(c * log_sigmoid(jnp.full((8, RET_DV), dl_ref[1, h], F32)))


def _ret_tables(decay_logit):
    c = ROW_TILE
    return pl.pallas_call(
        _ret_table_kernel,
        in_specs=[pl.BlockSpec(memory_space=pltpu.SMEM)],
        out_shape=(jax.ShapeDtypeStruct((RET_HEADS, c, c), F32),
                   jax.ShapeDtypeStruct((RET_HEADS, 4, c, RET_ROW_LANES), F32),
                   jax.ShapeDtypeStruct((RET_HEADS, 2, 8, RET_DV), F32)),
    )(decay_logit)


def _lanes(t, width):
    return jnp.concatenate([t] * (width // RET_ROW_LANES), axis=1)


def _ret_heads(q_ref, k_ref, v_ref, k_col=0, v_col=0):
    return [(q_ref[:, h * RET_DK:(h + 1) * RET_DK], k_ref[:, k_col + h * RET_DK:k_col + (h + 1) * RET_DK],
             v_ref[:, v_col + h * RET_DV:v_col + (h + 1) * RET_DV]) for h in range(RET_HEADS)]


def _ret_fwd_kernel(c_ref, x_ref, mod_ref, g_ref, w_ref, cs_ref, mask_ref, row_ref, cd_ref, p_ref, o_ref, s_ref):
    @pl.when(pl.program_id(1) == 0)
    def _():
        s_ref[...] = jnp.zeros_like(s_ref)

    xt = jnp.where(pl.program_id(1) == 0, c_ref[...], x_ref[...])
    h_in = _norm_mod(xt, g_ref[...], mod_ref[...]).astype(BF16)
    half = RET_DK // 2
    k_col = RET_HEADS * RET_DK
    v_col = 2 * k_col
    z_col = v_col + RET_HEADS * RET_DV
    cos = cs_ref[:, :half]
    sin = cs_ref[:, half:]
    for c0 in range(0, v_col, RET_DK):
        acc = jnp.dot(h_in, w_ref[:, c0:c0 + RET_DK], preferred_element_type=F32)
        t1, t2 = acc[:, :half], acc[:, half:]
        p_ref[:, c0:c0 + half] = (t1 * cos - t2 * sin).astype(BF16)
        p_ref[:, c0 + half:c0 + RET_DK] = (t1 * sin + t2 * cos).astype(BF16)
    for c0 in range(v_col, z_col, PROJ_COLS):
        p_ref[:, c0:c0 + PROJ_COLS] = jnp.dot(h_in, w_ref[:, c0:c0 + PROJ_COLS],
                                              preferred_element_type=F32).astype(BF16)

    def gate_cols(h):
        c0 = z_col + h * RET_DV
        acc = jnp.dot(h_in, w_ref[:, c0:c0 + RET_DV], preferred_element_type=F32)
        p_ref[:, c0:c0 + RET_DV] = _silu(acc).astype(BF16)

    qkv = _ret_heads(p_ref, p_ref, p_ref, k_col, v_col)
    sc = {}

    def scores(h):
        q, k, _ = qkv[h]
        sc[h] = lax.dot_general(q, k, _NT, preferred_element_type=F32)

    def finish(h):
        q, k, v = qkv[h]
        s = s_ref[h]
        qs = _bdot(q, s)
        kd = k.astype(F32) * _lanes(row_ref[h, 1], RET_DK)
        s_ref[h] = s * cd_ref[h, 0, 0:1, :] + _bdot(kd, v, _TN)
        o = _bdot(sc[h] * mask_ref[h], v)
        o_ref[:, h * RET_DV:(h + 1) * RET_DV] = (o + qs * _lanes(row_ref[h, 0], RET_DV)).astype(BF16)

    scores(0)
    for h in range(RET_HEADS):
        if h + 1 < RET_HEADS:
            scores(h + 1)
        gate_cols(h)
        finish(h)


def _ret_bwd_kernel(q_ref, k_ref, v_ref, z_ref, op_ref, c_ref, x_ref, mod_ref, wo_ref, row_ref, cd_ref, fg_ref,
                    xo_ref, s_ref, *, final):
    @pl.when(pl.program_id(1) == 0)
    def _():
        s_ref[...] = jnp.zeros_like(s_ref)

    qkv = _ret_heads(q_ref, k_ref, v_ref)
    gz = {}

    def inter(h):
        q, k, v = qkv[h]
        vv = slice(h * RET_DV, (h + 1) * RET_DV)
        s = s_ref[h]
        o = op_ref[:, vv].astype(F32) + _bdot(q, s) * _lanes(row_ref[h, 2], RET_DV)
        kd = k.astype(F32) * _lanes(row_ref[h, 3], RET_DK)
        s_ref[h] = s * cd_ref[h, 1, 0:1, :] + _bdot(kd, v, _TN)
        o = o * lax.rsqrt(jnp.mean(o * o, axis=-1, keepdims=True) + NORM_EPS)
        gz[h] = (o * z_ref[:, vv].astype(F32)).astype(BF16)

    acc = jnp.zeros((ROW_TILE, D), F32)
    inter(0)
    for h in range(RET_HEADS):
        if h + 1 < RET_HEADS:
            inter(h + 1)
        acc = acc + jnp.dot(gz[h], wo_ref[h * RET_DV:(h + 1) * RET_DV, :], preferred_element_type=F32)
    xt = jnp.where(pl.program_id(1) == 0, c_ref[...], x_ref[...])
    xn = xt + mod_ref[:, 2 * D:] * acc
    if final:
        xn = xn * lax.rsqrt(jnp.mean(xn * xn, axis=-1, keepdims=True) + NORM_EPS) * fg_ref[...]
    xo_ref[...] = xn


def _bwd_tile(tiles):
    return lambda s: jnp.where(s == 0, 0, tiles - s)


def _retention(ctx, x, mod4, g, w_in, rope, w_out, tables, layer, final_g):
    n_batch = x.shape[0]
    x_skip = 0 if x is ctx else ctx.shape[1] // ROW_TILE
    tiles = x.shape[1] // ROW_TILE + x_skip
    lc = tiles * ROW_TILE
    width = RET_HEADS * RET_DV
    state = pltpu.VMEM((RET_HEADS, RET_DK, RET_DV), F32)
    mask, row, cd = tables
    mask_spec = pl.BlockSpec(mask.shape, lambda b, s: (0, 0, 0))
    row_spec = pl.BlockSpec(row.shape, lambda b, s: (0, 0, 0, 0))
    cd_spec = pl.BlockSpec(cd.shape, lambda b, s: (0, 0, 0, 0))

    def col_spec(width_, col, tile_of_step):
        return pl.BlockSpec((None, ROW_TILE, width_), lambda b, s: (b, tile_of_step(s), col))

    def stream_specs(tile_of_step):
        return [pl.BlockSpec((None, ROW_TILE, D), lambda b, s: (b, 0, 0)),
                col_spec(D, 0, lambda s: jnp.maximum(tile_of_step(s) - x_skip, 0))]

    n_proj = w_in.shape[1]
    proj, o_part = pl.pallas_call(
        _ret_fwd_kernel,
        grid=(n_batch, tiles),
        in_specs=stream_specs(_identity) + [
                  _mod_spec(layer, n_batch, _identity),
                  pl.BlockSpec((1, D), lambda b, s: (0, 0)), pl.BlockSpec((D, n_proj), lambda b, s: (0, 0)),
                  pl.BlockSpec((ROW_TILE, RET_DK), lambda b, s: (s, 0)), mask_spec, row_spec, cd_spec],
        out_specs=[col_spec(n_proj, 0, _identity), col_spec(width, 0, _identity)],
        out_shape=[jax.ShapeDtypeStruct((n_batch, lc, n_proj), BF16),
                   jax.ShapeDtypeStruct((n_batch, lc, width), BF16)],
        scratch_shapes=[state],
        compiler_params=_params("parallel", "arbitrary"),
    )(ctx, x, mod4, g.reshape(1, D), w_in, rope, mask, row, cd)

    bt = _bwd_tile(tiles)
    final = final_g is not None
    if final:
        ctx_tiles = 1
        out_spec = col_spec(D, 0, lambda s: bt(jnp.maximum(s, 1)) - ctx_tiles)
        out_shape = jax.ShapeDtypeStruct((n_batch, lc - ctx_tiles * ROW_TILE, D), F32)
        fg = final_g.reshape(1, D)
    else:
        out_spec = col_spec(D, 0, bt)
        out_shape = jax.ShapeDtypeStruct((n_batch, lc, D), F32)
        fg = jnp.ones((1, D), F32)
    return pl.pallas_call(
        functools.partial(_ret_bwd_kernel, final=final),
        grid=(n_batch, tiles),
        in_specs=[col_spec(RET_HEADS * RET_DK, 0, bt), col_spec(RET_HEADS * RET_DK, 1, bt),
                  col_spec(width, 1, bt), col_spec(width, 2, bt), col_spec(width, 0, bt)] + stream_specs(bt) + [
                  _mod_spec(layer, n_batch, bt),
                  pl.BlockSpec((width, D), lambda b, s: (0, 0)), row_spec, cd_spec,
                  pl.BlockSpec((1, D), lambda b, s: (0, 0))],
        out_specs=out_spec,
        out_shape=out_shape,
        scratch_shapes=[state],
        compiler_params=_params("parallel", "arbitrary"),
    )(proj, proj, proj, proj, o_part, ctx, x, mod4, w_out, row, cd, fg)


def _gm_kernel(x_ref, mod_ref, g_ref, w_ref, vg_ref, ws_ref, bs_ref, wo_ref, xo_ref, v_ref):
    mod = mod_ref[...]
    h = _norm_mod(x_ref[...], g_ref[...], mod).astype(BF16)
    for c0 in range(0, GM_WIDTH, PROJ_COLS):
        v_ref[:, c0:c0 + PROJ_COLS] = jnp.dot(h, w_ref[:, GM_WIDTH + c0:GM_WIDTH + c0 + PROJ_COLS],
                                              preferred_element_type=F32)
    v = v_ref[...]
    v = v - jnp.mean(v, axis=-1, keepdims=True)
    vn = (v * lax.rsqrt(jnp.mean(v * v, axis=-1, keepdims=True) + NORM_EPS) * vg_ref[...]).astype(BF16)

    c = GM_CHUNK
    gw = GM_WIDTH // GM_GROUPS
    gated = {}

    def group(g):
        cols = slice(g * gw, (g + 1) * gw)
        u = jnp.dot(h, w_ref[:, cols], preferred_element_type=F32)
        z = jnp.dot(h, w_ref[:, 2 * GM_WIDTH + g * gw:2 * GM_WIDTH + (g + 1) * gw], preferred_element_type=F32)
        mixed = jnp.concatenate([jnp.dot(ws_ref[g], vn[ci * c:(ci + 1) * c, cols], preferred_element_type=F32)
                                 for ci in range(ROW_TILE // c)], axis=0)
        mixed = mixed + jnp.concatenate([bs_ref[:, g:g + 1]] * (ROW_TILE // c), axis=0)
        gated[g] = (u * mixed * _silu(z)).astype(BF16)

    acc = jnp.zeros((ROW_TILE, D), F32)
    group(0)
    for g in range(GM_GROUPS):
        if g + 1 < GM_GROUPS:
            group(g + 1)
        acc = acc + jnp.dot(gated[g], wo_ref[g * gw:(g + 1) * gw, :], preferred_element_type=F32)
    xo_ref[...] = x_ref[...] + mod[:, 2 * D:] * acc


def _gmlp(xc, mod4, g, w_in, vnorm_g, w_s, b_s, w_out, layer):
    n_batch, lc, _ = xc.shape
    tiles = lc // ROW_TILE
    tile_spec = pl.BlockSpec((None, ROW_TILE, D), lambda b, t: (b, t, 0))

    def full(a):
        return pl.BlockSpec(a.shape, lambda b, t: (0,) * a.ndim)

    return pl.pallas_call(
        _gm_kernel,
        grid=(n_batch, tiles),
        in_specs=[tile_spec, _mod_spec(layer, n_batch, _identity), full(g), full(w_in), full(vnorm_g), full(w_s),
                  full(b_s), full(w_out)],
        out_specs=tile_spec,
        out_shape=jax.ShapeDtypeStruct(xc.shape, F32),
        scratch_shapes=[pltpu.VMEM((ROW_TILE, GM_WIDTH), F32)],
        compiler_params=_params("parallel", "parallel"),
    )(xc, mod4, g, w_in, vnorm_g, w_s, b_s, w_out)


def _head_sum(x, e_ref):
    outs = []
    e = e_ref[...]
    for g in range(x.shape[-1] // WKV_LANES):
        xs = x[:, g * WKV_LANES:(g + 1) * WKV_LANES]
        hi = xs.astype(BF16)
        lo = (xs - hi.astype(F32)).astype(BF16)
        outs.append(jnp.dot(hi, e, preferred_element_type=F32) + jnp.dot(lo, e, preferred_element_type=F32))
    return jnp.concatenate(outs, axis=1)


def _rw_prep_kernel(xp_ref, x_ref, xn_ref, mod_ref, g_ref, mu_ref, wm_ref, w1_ref, w2_ref, a1_ref, a2_ref,
                    vec_ref, e_ref, r_ref, v_ref, kk_ref, k_ref, b_ref, ld_ref, bonus_ref, z_ref,
                    *, grid_rows):
    t = pl.program_id(1)
    is_ctx = t == 0
    g, mod = g_ref[...], mod_ref[...]
    h = _norm_mod(x_ref[...], g, mod)
    h_up = _norm_mod(xp_ref[...], g, mod)
    h_dn = _norm_mod(xn_ref[...], g, mod)
    n, q = ROW_TILE, D // 4
    i = lax.broadcasted_iota(jnp.int32, (n, q), 0)
    grow = (t - 1) * (n // GRID_W) + i // GRID_W
    pos = jnp.where(is_ctx, i, i % GRID_W)
    has_prev = pos > 0
    has_next = pos < jnp.where(is_ctx, n - 1, GRID_W - 1)
    hq = [h[:, j * q:(j + 1) * q] for j in range(4)]

    def prev_of(x):
        return jnp.where(has_prev, pltpu.roll(x, 1, 0), 0.0)

    def next_of(x):
        return jnp.where(has_next, pltpu.roll(x, n - 1, 0), 0.0)

    up = jnp.concatenate([h_up[:, 2 * q:3 * q], hq[2][:n - GRID_W]], axis=0)
    down = jnp.concatenate([hq[3][GRID_W:], h_dn[:, 3 * q:]], axis=0)
    up = jnp.where(grow > 0, up, 0.0)
    down = jnp.where(grow < grid_rows - 1, down, 0.0)
    shifted = jnp.concatenate([prev_of(hq[0]),
                               jnp.where(is_ctx, prev_of(hq[1]), next_of(hq[1])),
                               jnp.where(is_ctx, next_of(hq[2]), up),
                               jnp.where(is_ctx, next_of(hq[3]), down)], axis=1)
    xx = shifted - h

    hb, xb = h.astype(BF16), xx.astype(BF16)

    def mix(p):
        return hb + xb * mu_ref[p:p + 1, :].astype(BF16)

    vec = vec_ref[...]
    r = jnp.dot(mix(0), wm_ref[0], preferred_element_type=F32)
    k = jnp.dot(mix(2), wm_ref[1], preferred_element_type=F32)
    v = jnp.dot(mix(3), wm_ref[2], preferred_element_type=F32)
    z_ref[...] = jnp.dot(mix(5), wm_ref[3], preferred_element_type=F32).astype(BF16)
    tw = jnp.tanh(jnp.dot(mix(1), w1_ref[...], preferred_element_type=F32))
    ta = jnp.dot(mix(4), a1_ref[...], preferred_element_type=F32)

    kk = k * vec[4:5]
    kk = kk / jnp.maximum(jnp.sqrt(_head_sum(kk * kk, e_ref)), 1e-12)
    r_ref[...] = r
    v_ref[...] = v
    kk_ref[...] = kk
    bonus = jnp.zeros_like(r)
    for d in range(2):
        lw = vec[d:d + 1] + _bdot(tw, w2_ref[d])
        half_decay = -0.5 * math.exp(-0.5)
        ld_ref[d] = half_decay + half_decay * jnp.tanh(0.5 * lw)
        a = 0.5 + 0.5 * jnp.tanh(0.5 * (vec[2 + d:3 + d] + _bdot(ta, a2_ref[d])))
        kd = k * (1.0 + (a - 1.0) * vec[5:6])
        k_ref[d] = kd
        b_ref[d] = kk * a
        bonus = bonus + r * kd * vec[6:7]
    bonus_ref[...] = (_head_sum(bonus, e_ref) * v).astype(BF16)


def _rw_prepare(xc, mod4, g, mu, wm, w1, w2, a1, a2, vec, e, layer, grid_rows):
    n_batch, lc, _ = xc.shape
    tiles = lc // ROW_TILE
    per_tile = ROW_TILE // GRID_W
    last = lc // GRID_W - 1

    def full(a):
        return pl.BlockSpec(a.shape, lambda b, t: (0,) * a.ndim)

    tile_spec = pl.BlockSpec((None, ROW_TILE, D), lambda b, t: (b, t, 0))
    dir_spec = pl.BlockSpec((2, None, ROW_TILE, D), lambda b, t: (0, b, t, 0))
    one = jax.ShapeDtypeStruct((n_batch, lc, D), F32)
    two = jax.ShapeDtypeStruct((2, n_batch, lc, D), F32)
    half = jax.ShapeDtypeStruct((n_batch, lc, D), BF16)
    return pl.pallas_call(
        functools.partial(_rw_prep_kernel, grid_rows=grid_rows),
        grid=(n_batch, tiles),
        in_specs=[pl.BlockSpec((None, GRID_W, D), lambda b, t: (b, jnp.maximum(t * per_tile - 1, 0), 0)),
                  tile_spec,
                  pl.BlockSpec((None, GRID_W, D), lambda b, t: (b, jnp.minimum((t + 1) * per_tile, last), 0)),
                  _mod_spec(layer, n_batch, _identity), full(g), full(mu), full(wm), full(w1), full(w2),
                  full(a1), full(a2), full(vec), full(e)],
        out_specs=[tile_spec, tile_spec, tile_spec, dir_spec, dir_spec, dir_spec, tile_spec, tile_spec],
        out_shape=[one, one, one, two, two, two, half, half],
        compiler_params=_params("parallel", "parallel"),
    )(xc, xc, xc, mod4, g, mu, wm, w1, w2, a1, a2, vec, e)


def _wkv_masks(reverse):
    c, hg = WKV_CHUNK, WKV_GROUP
    t = lax.broadcasted_iota(jnp.int32, (c, hg * c), 0)
    s = lax.broadcasted_iota(jnp.int32, (c, hg * c), 1) % c
    strict = jnp.where((s > t) if reverse else (s < t), 1.0, 0.0)
    incl = jnp.where((s >= t) if reverse else (s <= t), 1.0, 0.0)
    eye = jnp.where(s == t, 1.0, 0.0)
    return strict, jnp.concatenate([incl, incl], axis=1), eye, jnp.concatenate([incl[:, :c]] * 3, axis=1).astype(BF16)


def _wkv_stages(chains, m4, bd, bd_state):
    c, w, hg = WKV_CHUNK, WKV_LANES, WKV_GROUP
    n = range(len(chains))
    refs, rows, masks, reverse = zip(*[(ch["refs"], (ch["rows"], ch["lanes"]), ch["masks"], ch["reverse"])
                                       for ch in chains])
    strict, incl2, eye, cum_mat = zip(*masks)

    def dot(a, b, dims=None):
        if dims is None:
            return jnp.dot(a, b, preferred_element_type=F32)
        return lax.dot_general(a, b, dims, preferred_element_type=F32)

    def stack(x):
        return jnp.where(jnp.concatenate([m4] * (x.shape[1] // w), axis=1), jnp.concatenate([x] * hg, axis=0), 0)

    def blockdiag(x):
        return jnp.where(bd, jnp.concatenate([x.astype(BF16)] * hg, axis=0), 0)

    ld = [refs[i][5][rows[i]] for i in n]
    cum = [dot(cum_mat[i], jnp.concatenate(_split3(ld[i]), axis=0)) for i in n]
    yield
    tot = [cum[i][0:1] if reverse[i] else cum[i][c - 1:c] for i in n]
    e_out = [jnp.exp(-cum[i]) for i in n]
    vb = [refs[i][4][rows[i]].astype(BF16) for i in n]
    rt = [(refs[i][0][rows[i]] * jnp.exp(cum[i])).astype(BF16) for i in n]
    at = [(-refs[i][1][rows[i]] * jnp.exp(cum[i] - ld[i])).astype(BF16) for i in n]
    kt = [(refs[i][3][rows[i]] * e_out[i]).astype(BF16) for i in n]
    bt = [(refs[i][2][rows[i]] * e_out[i]).astype(BF16) for i in n]

    sc = [dot(jnp.concatenate([at[i], rt[i]], axis=0), jnp.concatenate([stack(kt[i]), stack(bt[i])], axis=0), _NT)
          for i in n]
    yield
    a_ak = [(sc[i][:c, :w] * strict[i]).astype(BF16) for i in n]
    p = [sc[i][:c, w:] * strict[i] for i in n]
    a_r = [(sc[i][c:] * incl2[i]).astype(BF16) for i in n]
    tinv = [eye[i] + p[i] for i in n]
    for _ in range(c.bit_length() - 2):
        p = [dot(p[i].astype(BF16), blockdiag(p[i])) for i in n]
        yield
        tinv = [tinv[i] + dot(p[i].astype(BF16), blockdiag(tinv[i])) for i in n]
        yield
    sv = [stack(vb[i]) for i in n]
    akv = [dot(a_ak[i], sv[i]).astype(BF16) for i in n]
    yield
    tx = [dot(tinv[i].astype(BF16), stack(jnp.concatenate([at[i], akv[i]], axis=1))) for i in n]
    yield

    ht = [ch["ht"][ch["slot"]] for ch in chains]
    wrh = [dot(jnp.concatenate([tx[i][:, :w].astype(BF16), rt[i]], axis=0), ht[i].astype(BF16), _NT) for i in n]
    yield
    ub = [(wrh[i][:c] + tx[i][:, w:]).astype(BF16) for i in n]
    y = [wrh[i][c:] + dot(a_r[i], jnp.concatenate([sv[i], stack(ub[i])], axis=0)) for i in n]
    yield
    e_end = [jnp.exp(tot[i] - cum[i]) for i in n]
    kb = [jnp.concatenate([(refs[i][3][rows[i]] * e_end[i]).astype(BF16),
                           (refs[i][2][rows[i]] * e_end[i]).astype(BF16)], axis=0) for i in n]
    upd = [dot(jnp.concatenate([vb[i], ub[i]], axis=0), kb[i], _TN) for i in n]
    for i, ch in enumerate(chains):
        ch["y"][rows[i]] = y[i].astype(BF16)
        ch["ht"][ch["slot"]] = jnp.where(bd_state, ht[i] * jnp.exp(tot[i]) + upd[i], 0.0)


def _wkv_kernel(rf, kkf, bf, kf, vf, ldf, rb, kkb, bb, kb, vb, ldb, yf_ref, yb_ref, ht_ref):
    @pl.when(pl.program_id(1) == 0)
    def _():
        ht_ref[...] = jnp.zeros_like(ht_ref)

    c, w, hg = WKV_CHUNK, WKV_LANES, WKV_GROUP
    n_chunks = ROW_TILE // c
    lane_head = lax.broadcasted_iota(jnp.int32, (hg * c, w), 1) // RW_HEAD
    row_head = lax.broadcasted_iota(jnp.int32, (hg * c, w), 0) // c
    m4 = lane_head == row_head
    bd = lax.broadcasted_iota(jnp.int32, (hg * c, hg * c), 1) // c == row_head
    bd_state = (lax.broadcasted_iota(jnp.int32, (w, w), 0) // RW_HEAD
                == lax.broadcasted_iota(jnp.int32, (w, w), 1) // RW_HEAD)
    masks_f, masks_b = _wkv_masks(False), _wkv_masks(True)
    groups = D // w

    def chunk(step):
        rows_f = slice(step * c, (step + 1) * c)
        rows_b = slice((n_chunks - 1 - step) * c, (n_chunks - step) * c)
        chains = []
        for g in range(groups):
            lanes = slice(g * w, (g + 1) * w)
            chains.append(dict(refs=(rf, kkf, bf, kf, vf, ldf), rows=rows_f, lanes=lanes, masks=masks_f,
                               reverse=False, ht=ht_ref, slot=g, y=yf_ref))
            chains.append(dict(refs=(rb, kkb, bb, kb, vb, ldb), rows=rows_b, lanes=lanes, masks=masks_b,
                               reverse=True, ht=ht_ref, slot=groups + g, y=yb_ref))
        return _wkv_stages(chains, m4, bd, bd_state)

    chunks = [chunk(step) for step in range(n_chunks)]
    for _ in range(WKV_SETUP_STAGES):
        next(chunks[0])
    for step in range(n_chunks):
        for k in range(WKV_INVERSE_STAGES + WKV_APPLY_STAGES):
            next(chunks[step], None)
            if step + 1 < n_chunks and k in WKV_SETUP_SLOTS:
                next(chunks[step + 1])


def _wkv(r, kk, b, k, v, ld):
    n_batch, lc, _ = r.shape
    tiles = lc // ROW_TILE
    bt = _bwd_tile(tiles)

    def specs(tile_of, direction):
        one = pl.BlockSpec((None, ROW_TILE, D), lambda bb, s: (bb, tile_of(s), 0))
        two = pl.BlockSpec((None, None, ROW_TILE, D), lambda bb, s: (direction, bb, tile_of(s), 0))
        return one, two

    f1, f2 = specs(_identity, 0)
    b1, b2 = specs(bt, 1)
    return pl.pallas_call(
        _wkv_kernel,
        grid=(n_batch, tiles),
        in_specs=[f1, f1, f2, f2, f1, f2, b1, b1, b2, b2, b1, b2],
        out_specs=[f1, b1],
        out_shape=[jax.ShapeDtypeStruct(r.shape, BF16)] * 2,
        scratch_shapes=[pltpu.VMEM((2 * (D // WKV_LANES), WKV_LANES, WKV_LANES), F32)],
        compiler_params=_params("parallel", "arbitrary"),
    )(r, kk, b, k, v, ld, r, kk, b, k, v, ld)


def _rw_out_kernel(yf_ref, yb_ref, bonus_ref, z_ref, x_ref, mod_ref, vec_ref, e_ref, wo_ref, xo_ref):
    y = yf_ref[...].astype(F32) + yb_ref[...].astype(F32)
    inv = 1.0 / RW_HEAD
    yc = y - _head_sum(y, e_ref) * inv
    yn = yc * lax.rsqrt(_head_sum(yc * yc, e_ref) * inv + RW_LNX_EPS)
    yn = yn * vec_ref[0:1] + vec_ref[1:2]
    o = (yn + bonus_ref[...].astype(F32)) * _silu(z_ref[...].astype(F32))
    xo_ref[...] = x_ref[...] + mod_ref[:, 2 * D:] * _bdot(o, wo_ref[...])


def _rw_output(xc, y_f, y_b, bonus, z, mod4, vec, e, w_out, layer):
    n_batch, lc, _ = xc.shape
    tiles = lc // ROW_TILE
    tile_spec = pl.BlockSpec((None, ROW_TILE, D), lambda b, t: (b, t, 0))

    def full(a):
        return pl.BlockSpec(a.shape, lambda b, t: (0,) * a.ndim)

    return pl.pallas_call(
        _rw_out_kernel,
        grid=(n_batch, tiles),
        in_specs=[tile_spec, tile_spec, tile_spec, tile_spec, tile_spec, _mod_spec(layer, n_batch, _identity),
                  full(vec), full(e), full(w_out)],
        out_specs=tile_spec,
        out_shape=jax.ShapeDtypeStruct(xc.shape, F32),
        compiler_params=_params("parallel", "parallel"),
    )(y_f, y_b, bonus, z, xc, mod4, vec, e, w_out)


def _rwkv(xc, mod4, g, mu, w_rkvg, w0, w1, w2, a0, a1, a2, k_k, k_a, r_k, lnx_g, lnx_b, w_out, layer, grid_rows):
    zeros = jnp.zeros((RW_LORA, D), F32)
    w1c = jnp.concatenate([w1[0], w1[1]], axis=1).astype(BF16)
    a1c = jnp.concatenate([a1[0], a1[1]], axis=1).astype(BF16)
    w2p = jnp.stack([jnp.concatenate([w2[0], zeros]), jnp.concatenate([zeros, w2[1]])]).astype(BF16)
    a2p = jnp.stack([jnp.concatenate([a2[0], zeros]), jnp.concatenate([zeros, a2[1]])]).astype(BF16)
    zrow = jnp.zeros((D,), F32)
    vec = jnp.stack([w0[0], w0[1], a0[0], a0[1], k_k, k_a, r_k.reshape(D), zrow])
    lane = jnp.arange(WKV_LANES) // RW_HEAD
    e = (lane[:, None] == lane[None, :]).astype(BF16)
    r, v, kk, k, b, ld, bonus, z = _rw_prepare(xc, mod4, g.reshape(1, D), mu, w_rkvg.astype(BF16), w1c, w2p,
                                               a1c, a2p, vec, e, layer, grid_rows)
    y_f, y_b = _wkv(r, kk, b, k, v, ld)
    vec_o = jnp.stack([lnx_g, lnx_b] + [zrow] * 6)
    return _rw_output(xc, y_f, y_b, bonus, z, mod4, vec_o, e, w_out.astype(BF16), layer)


def kernel(x, c, ctx, c_ctx, ada_w, ada_b, norm_g, final_g, ret_w_in, ret_decay, ret_w_out, gm_w_in, gm_vnorm_g,
           gm_w_s, gm_b_s, gm_w_out, rw_mu, rw_w_rkvg, rw_w0, rw_w1, rw_w2, rw_a0, rw_a1, rw_a2, rw_k_k, rw_k_a,
           rw_r_k, rw_lnx_g, rw_lnx_b, rw_w_out):
    n_batch, seq_len, _ = x.shape
    ctx_len = ctx.shape[1]
    depth = ada_w.shape[0]
    assert ctx_len == ROW_TILE and seq_len % ROW_TILE == 0 and n_batch < 8

    cvec = jnp.zeros((8, D), F32).at[:n_batch].set(c).at[n_batch].set(c_ctx)
    mod4 = _modulation(cvec, ada_w, ada_b).reshape(depth, 8, 1, 3 * D)
    rope = _rope_table(ctx_len, seq_len)
    k_scale = jnp.concatenate([jnp.ones((RET_HEADS * RET_DK,), F32),
                               jnp.full((RET_HEADS * RET_DK,), RET_DK ** -0.5, F32),
                               jnp.ones((2 * RET_HEADS * RET_DV,), F32)])

    xc = None
    for i in range(depth):
        kind, j = i % N_MIXERS, i // N_MIXERS
        if kind == 0:
            w_in = (ret_w_in[j] * k_scale).astype(BF16)
            stream = (ctx, x) if i == 0 else (xc, xc)
            xc = _retention(*stream, mod4, norm_g[i], w_in, rope, ret_w_out[j].astype(BF16), _ret_tables(ret_decay[j]), i,
                            final_g if i == depth - 1 else None)
        elif kind == 1:
            xc = _gmlp(xc, mod4, norm_g[i].reshape(1, D), gm_w_in[j].astype(BF16), gm_vnorm_g[j].reshape(1, GM_WIDTH),
                       gm_w_s[j].astype(BF16), gm_b_s[j].T, gm_w_out[j].astype(BF16), i)
        else:
            xc = _rwkv(xc, mod4, norm_g[i], rw_mu[j], rw_w_rkvg[j], rw_w0[j], rw_w1[j], rw_w2[j], rw_a0[j],
                       rw_a1[j], rw_a2[j], rw_k_k[j], rw_k_a[j], rw_r_k[j], rw_lnx_g[j], rw_lnx_b[j],
                       rw_w_out[j], i, seq_len // GRID_W)
    assert (depth - 1) % N_MIXERS == 0, "the final norm is fused into a retention layer"
    return xc
```

```python
import functools
import math

import jax
import jax.numpy as jnp
from jax import lax
from jax.experimental import pallas as pl
from jax.experimental.pallas import tpu as pltpu

F32 = jnp.float32
BF16 = jnp.bfloat16

D = 1024
GRID_W = 64
NORM_EPS = 1e-6
N_MIXERS = 3
ROW_TILE = 256
RET_HEADS = 4
RET_DK = 256
RET_DV = 512
ROPE_BASE = 10000.0
GM_WIDTH = 2 * D
GM_GROUPS = 8
GM_CHUNK = 128
RW_HEAD = 64
RW_LORA = 64
RW_LNX_EPS = 64e-5
WKV_CHUNK = 64
WKV_LANES = 256
WKV_GROUP = WKV_LANES // RW_HEAD
LANE_TILE = 128
WKV_SETUP_STAGES, WKV_INVERSE_STAGES, WKV_APPLY_STAGES = 2, 10, 5
WKV_SETUP_SLOTS = (10, 12)
VMEM_LIMIT = 56 * 1024 * 1024

_NT = (((1,), (1,)), ((), ()))
_TN = (((0,), (0,)), ((), ()))


def _params(*sem):
    return pltpu.CompilerParams(dimension_semantics=sem, vmem_limit_bytes=VMEM_LIMIT)


def _silu(x):
    return x / (1.0 + jnp.exp(-x))


def _bdot(a, b, dims=None):
    a = a.astype(BF16)
    b = b.astype(BF16)
    if dims is None:
        return jnp.dot(a, b, preferred_element_type=F32)
    return lax.dot_general(a, b, dims, preferred_element_type=F32)


def _split2(x):
    hi = x.astype(BF16)
    return hi, (x - hi.astype(F32)).astype(BF16)


def _norm_mod(x, g, mod):
    ms = jnp.mean(x * x, axis=-1, keepdims=True)
    y = x * lax.rsqrt(ms + NORM_EPS) * g
    return y * (1.0 + mod[:, D:2 * D]) + mod[:, :D]


def _mod_spec(layer, n_batch, tile_of_step):
    def index(b, s):
        t = tile_of_step(s)
        return (layer, jnp.where(t == 0, n_batch, b), 0, 0)
    return pl.BlockSpec((None, None, 1, 3 * D), index)


def _identity(s):
    return s


def _mod_kernel(c_ref, w_ref, b_ref, o_ref):
    s = _silu(c_ref[...])
    o_ref[...] = jnp.dot(s, w_ref[...], preferred_element_type=F32,
                         precision=lax.Precision.HIGHEST) + b_ref[...]


def _modulation(cvec, ada_w, ada_b):
    depth = ada_w.shape[0]
    return pl.pallas_call(
        _mod_kernel,
        grid=(depth, 3),
        in_specs=[pl.BlockSpec((8, D), lambda i, j: (0, 0)),
                  pl.BlockSpec((None, D, D), lambda i, j: (i, 0, j)),
                  pl.BlockSpec((None, 1, D), lambda i, j: (i, 0, j))],
        out_specs=pl.BlockSpec((None, 8, D), lambda i, j: (i, 0, j)),
        out_shape=jax.ShapeDtypeStruct((depth, 8, 3 * D), F32),
        compiler_params=_params("arbitrary", "arbitrary"),
    )(cvec, ada_w, ada_b.reshape(depth, 1, 3 * D))


PROJ_COLS = 512


def _rope_table(ctx_len, seq_len):
    n_freq = RET_DK // 4
    freqs = ROPE_BASE ** (-jnp.arange(n_freq, dtype=F32) / n_freq)
    t = jnp.arange(seq_len, dtype=jnp.int32)
    row, col = (t // GRID_W).astype(F32), (t % GRID_W).astype(F32)
    ang = jnp.concatenate([row[:, None] * freqs, col[:, None] * freqs], axis=-1)
    ang = jnp.concatenate([jnp.zeros((ctx_len, RET_DK // 2), F32), ang], axis=0)
    return jnp.concatenate([jnp.cos(ang), jnp.sin(ang)], axis=-1)


RET_ROW_LANES = 128


def _ret_table_kernel(dl_ref, mask_ref, row_ref, cd_ref):
    c = ROW_TILE
    i = lax.broadcasted_iota(jnp.int32, (c, c), 0).astype(F32)
    j = lax.broadcasted_iota(jnp.int32, (c, c), 1).astype(F32)
    ir = i[:, :RET_ROW_LANES]

    def log_sigmoid(x):
        return jnp.minimum(x, 0.0) - jnp.log(1.0 + jnp.exp(-jnp.abs(x)))

    for h in range(RET_HEADS):
        lg_f = log_sigmoid(jnp.full((c, c), dl_ref[0, h], F32))
        lg_b = log_sigmoid(jnp.full((c, c), dl_ref[1, h], F32))
        fwd = jnp.where(i >= j, jnp.exp(jnp.maximum(i - j, 0.0) * lg_f), 0.0)
        bwd = jnp.where(j >= i, jnp.exp(jnp.maximum(j - i, 0.0) * lg_b), 0.0)
        mask_ref[h] = fwd + bwd
        lf = log_sigmoid(jnp.full((c, RET_ROW_LANES), dl_ref[0, h], F32))
        lb = log_sigmoid(jnp.full((c, RET_ROW_LANES), dl_ref[1, h], F32))
        row_ref[h, 0] = jnp.exp((ir + 1.0) * lf)
        row_ref[h, 1] = jnp.exp((c - 1.0 - ir) * lf)
        row_ref[h, 2] = jnp.exp((c - ir) * lb)
        row_ref[h, 3] = jnp.exp(ir * lb)
        cd_ref[h, 0] = jnp.exp(c * log_sigmoid(jnp.full((8, RET_DV), dl_ref[0, h], F32)))
        cd_ref[h, 1] = jnp.exp(c * log_sigmoid(jnp.full((8, RET_DV), dl_ref[1, h], F32)))


def _ret_tables(decay_logit):
    c = ROW_TILE
    return pl.pallas_call(
        _ret_table_kernel,
        in_specs=[pl.BlockSpec(memory_space=pltpu.SMEM)],
        out_shape=(jax.ShapeDtypeStruct((RET_HEADS, c, c), F32),
                   jax.ShapeDtypeStruct((RET_HEADS, 4, c, RET_ROW_LANES), F32),
                   jax.ShapeDtypeStruct((RET_HEADS, 2, 8, RET_DV), F32)),
    )(decay_logit)


def _lanes(t, width):
    return jnp.concatenate([t] * (width // RET_ROW_LANES), axis=1)


def _ret_heads(q_ref, k_ref, v_ref, k_col=0, v_col=0):
    return [(q_ref[:, h * RET_DK:(h + 1) * RET_DK], k_ref[:, k_col + h * RET_DK:k_col + (h + 1) * RET_DK],
             v_ref[:, v_col + h * RET_DV:v_col + (h + 1) * RET_DV]) for h in range(RET_HEADS)]


def _ret_fwd_kernel(c_ref, x_ref, mod_ref, g_ref, w_ref, cs_ref, mask_ref, row_ref, cd_ref, p_ref, o_ref, s_ref):
    @pl.when(pl.program_id(1) == 0)
    def _():
        s_ref[...] = jnp.zeros_like(s_ref)

    xt = jnp.where(pl.program_id(1) == 0, c_ref[...], x_ref[...])
    h_in = _norm_mod(xt, g_ref[...], mod_ref[...]).astype(BF16)
    half = RET_DK // 2
    k_col = RET_HEADS * RET_DK
    v_col = 2 * k_col
    z_col = v_col + RET_HEADS * RET_DV
    cos = cs_ref[:, :half]
    sin = cs_ref[:, half:]
    for c0 in range(0, v_col, RET_DK):
        acc = jnp.dot(h_in, w_ref[:, c0:c0 + RET_DK], preferred_element_type=F32)
        t1, t2 = acc[:, :half], acc[:, half:]
        p_ref[:, c0:c0 + half] = (t1 * cos - t2 * sin).astype(BF16)
        p_ref[:, c0 + half:c0 + RET_DK] = (t1 * sin + t2 * cos).astype(BF16)
    for c0 in range(v_col, z_col, PROJ_COLS):
        p_ref[:, c0:c0 + PROJ_COLS] = jnp.dot(h_in, w_ref[:, c0:c0 + PROJ_COLS],
                                              preferred_element_type=F32).astype(BF16)

    def gate_cols(h):
        c0 = z_col + h * RET_DV
        acc = jnp.dot(h_in, w_ref[:, c0:c0 + RET_DV], preferred_element_type=F32)
        p_ref[:, c0:c0 + RET_DV] = _silu(acc).astype(BF16)

    qkv = _ret_heads(p_ref, p_ref, p_ref, k_col, v_col)
    sc = {}

    def scores(h):
        q, k, _ = qkv[h]
        sc[h] = lax.dot_general(q, k, _NT, preferred_element_type=F32)

    def finish(h):
        q, k, v = qkv[h]
        s = s_ref[h]
        qs = _bdot(q, s)
        kd = k.astype(F32) * _lanes(row_ref[h, 1], RET_DK)
        s_ref[h] = s * cd_ref[h, 0, 0:1, :] + _bdot(kd, v, _TN)
        o = _bdot(sc[h] * mask_ref[h], v)
        o_ref[:, h * RET_DV:(h + 1) * RET_DV] = (o + qs * _lanes(row_ref[h, 0], RET_DV)).astype(BF16)

    scores(0)
    for h in range(RET_HEADS):
        if h + 1 < RET_HEADS:
            scores(h + 1)
        gate_cols(h)
        finish(h)


def _ret_bwd_kernel(q_ref, k_ref, v_ref, z_ref, op_ref, c_ref, x_ref, mod_ref, wo_ref, row_ref, cd_ref, fg_ref,
                    xo_ref, s_ref, *, final):
    @pl.when(pl.program_id(1) == 0)
    def _():
        s_ref[...] = jnp.zeros_like(s_ref)

    qkv = _ret_heads(q_ref, k_ref, v_ref)
    gz = {}

    def inter(h):
        q, k, v = qkv[h]
        vv = slice(h * RET_DV, (h + 1) * RET_DV)
        s = s_ref[h]
        o = op_ref[:, vv].astype(F32) + _bdot(q, s) * _lanes(row_ref[h, 2], RET_DV)
        kd = k.astype(F32) * _lanes(row_ref[h, 3], RET_DK)
        s_ref[h] = s * cd_ref[h, 1, 0:1, :] + _bdot(kd, v, _TN)
        o = o * lax.rsqrt(jnp.mean(o * o, axis=-1, keepdims=True) + NORM_EPS)
        gz[h] = (o * z_ref[:, vv].astype(F32)).astype(BF16)

    acc = jnp.zeros((ROW_TILE, D), F32)
    inter(0)
    for h in range(RET_HEADS):
        if h + 1 < RET_HEADS:
            inter(h + 1)
        acc = acc + jnp.dot(gz[h], wo_ref[h * RET_DV:(h + 1) * RET_DV, :], preferred_element_type=F32)
    xt = jnp.where(pl.program_id(1) == 0, c_ref[...], x_ref[...])
    xn = xt + mod_ref[:, 2 * D:] * acc
    if final:
        xn = xn * lax.rsqrt(jnp.mean(xn * xn, axis=-1, keepdims=True) + NORM_EPS) * fg_ref[...]
    xo_ref[...] = xn


def _bwd_tile(tiles):
    return lambda s: jnp.where(s == 0, 0, tiles - s)


def _retention(ctx, x, mod4, g, w_in, rope, w_out, tables, layer, final_g):
    n_batch = x.shape[0]
    x_skip = 0 if x is ctx else ctx.shape[1] // ROW_TILE
    tiles = x.shape[1] // ROW_TILE + x_skip
    lc = tiles * ROW_TILE
    width = RET_HEADS * RET_DV
    state = pltpu.VMEM((RET_HEADS, RET_DK, RET_DV), F32)
    mask, row, cd = tables
    mask_spec = pl.BlockSpec(mask.shape, lambda b, s: (0, 0, 0))
    row_spec = pl.BlockSpec(row.shape, lambda b, s: (0, 0, 0, 0))
    cd_spec = pl.BlockSpec(cd.shape, lambda b, s: (0, 0, 0, 0))

    def col_spec(width_, col, tile_of_step):
        return pl.BlockSpec((None, ROW_TILE, width_), lambda b, s: (b, tile_of_step(s), col))

    def stream_specs(tile_of_step):
        return [pl.BlockSpec((None, ROW_TILE, D), lambda b, s: (b, 0, 0)),
                col_spec(D, 0, lambda s: jnp.maximum(tile_of_step(s) - x_skip, 0))]

    n_proj = w_in.shape[1]
    proj, o_part = pl.pallas_call(
        _ret_fwd_kernel,
        grid=(n_batch, tiles),
        in_specs=stream_specs(_identity) + [
                  _mod_spec(layer, n_batch, _identity),
                  pl.BlockSpec((1, D), lambda b, s: (0, 0)), pl.BlockSpec((D, n_proj), lambda b, s: (0, 0)),
                  pl.BlockSpec((ROW_TILE, RET_DK), lambda b, s: (s, 0)), mask_spec, row_spec, cd_spec],
        out_specs=[col_spec(n_proj, 0, _identity), col_spec(width, 0, _identity)],
        out_shape=[jax.ShapeDtypeStruct((n_batch, lc, n_proj), BF16),
                   jax.ShapeDtypeStruct((n_batch, lc, width), BF16)],
        scratch_shapes=[state],
        compiler_params=_params("parallel", "arbitrary"),
    )(ctx, x, mod4, g.reshape(1, D), w_in, rope, mask, row, cd)

    bt = _bwd_tile(tiles)
    final = final_g is not None
    if final:
        ctx_tiles = 1
        out_spec = col_spec(D, 0, lambda s: bt(jnp.maximum(s, 1)) - ctx_tiles)
        out_shape = jax.ShapeDtypeStruct((n_batch, lc - ctx_tiles * ROW_TILE, D), F32)
        fg = final_g.reshape(1, D)
    else:
        out_spec = col_spec(D, 0, bt)
        out_shape = jax.ShapeDtypeStruct((n_batch, lc, D), F32)
        fg = jnp.ones((1, D), F32)
    return pl.pallas_call(
        functools.partial(_ret_bwd_kernel, final=final),
        grid=(n_batch, tiles),
        in_specs=[col_spec(RET_HEADS * RET_DK, 0, bt), col_spec(RET_HEADS * RET_DK, 1, bt),
                  col_spec(width, 1, bt), col_spec(width, 2, bt), col_spec(width, 0, bt)] + stream_specs(bt) + [
                  _mod_spec(layer, n_batch, bt),
                  pl.BlockSpec((width, D), lambda b, s: (0, 0)), row_spec, cd_spec,
                  pl.BlockSpec((1, D), lambda b, s: (0, 0))],
        out_specs=out_spec,
        out_shape=out_shape,
        scratch_shapes=[state],
        compiler_params=_params("parallel", "arbitrary"),
    )(proj, proj, proj, proj, o_part, ctx, x, mod4, w_out, row, cd, fg)


def _gm_kernel(x_ref, mod_ref, g_ref, w_ref, vg_ref, ws_ref, bs_ref, wo_ref, xo_ref, v_ref):
    mod = mod_ref[...]
    h = _norm_mod(x_ref[...], g_ref[...], mod).astype(BF16)
    for c0 in range(0, GM_WIDTH, PROJ_COLS):
        v_ref[:, c0:c0 + PROJ_COLS] = jnp.dot(h, w_ref[:, GM_WIDTH + c0:GM_WIDTH + c0 + PROJ_COLS],
                                              preferred_element_type=F32)
    v = v_ref[...]
    v = v - jnp.mean(v, axis=-1, keepdims=True)
    vn = (v * lax.rsqrt(jnp.mean(v * v, axis=-1, keepdims=True) + NORM_EPS) * vg_ref[...]).astype(BF16)

    c = GM_CHUNK
    gw = GM_WIDTH // GM_GROUPS
    gated = {}

    def group(g):
        cols = slice(g * gw, (g + 1) * gw)
        u = jnp.dot(h, w_ref[:, cols], preferred_element_type=F32)
        z = jnp.dot(h, w_ref[:, 2 * GM_WIDTH + g * gw:2 * GM_WIDTH + (g + 1) * gw], preferred_element_type=F32)
        mixed = jnp.concatenate([jnp.dot(ws_ref[g], vn[ci * c:(ci + 1) * c, cols], preferred_element_type=F32)
                                 for ci in range(ROW_TILE // c)], axis=0)
        mixed = mixed + jnp.concatenate([bs_ref[:, g:g + 1]] * (ROW_TILE // c), axis=0)
        gated[g] = (u * mixed * _silu(z)).astype(BF16)

    acc = jnp.zeros((ROW_TILE, D), F32)
    group(0)
    for g in range(GM_GROUPS):
        if g + 1 < GM_GROUPS:
            group(g + 1)
        acc = acc + jnp.dot(gated[g], wo_ref[g * gw:(g + 1) * gw, :], preferred_element_type=F32)
    xo_ref[...] = x_ref[...] + mod[:, 2 * D:] * acc


def _gmlp(xc, mod4, g, w_in, vnorm_g, w_s, b_s, w_out, layer):
    n_batch, lc, _ = xc.shape
    tiles = lc // ROW_TILE
    tile_spec = pl.BlockSpec((None, ROW_TILE, D), lambda b, t: (b, t, 0))

    def full(a):
        return pl.BlockSpec(a.shape, lambda b, t: (0,) * a.ndim)

    return pl.pallas_call(
        _gm_kernel,
        grid=(n_batch, tiles),
        in_specs=[tile_spec, _mod_spec(layer, n_batch, _identity), full(g), full(w_in), full(vnorm_g), full(w_s),
                  full(b_s), full(w_out)],
        out_specs=tile_spec,
        out_shape=jax.ShapeDtypeStruct(xc.shape, F32),
        scratch_shapes=[pltpu.VMEM((ROW_TILE, GM_WIDTH), F32)],
        compiler_params=_params("parallel", "parallel"),
    )(xc, mod4, g, w_in, vnorm_g, w_s, b_s, w_out)


def _head_sum(x, e_ref):
    outs = []
    e = e_ref[...]
    for g in range(x.shape[-1] // WKV_LANES):
        hi, lo = _split2(x[:, g * WKV_LANES:(g + 1) * WKV_LANES])
        outs.append(jnp.dot(hi, e, preferred_element_type=F32) + jnp.dot(lo, e, preferred_element_type=F32))
    return jnp.concatenate(outs, axis=1)


def _rw_prep_kernel(xp_ref, x_ref, xn_ref, mod_ref, g_ref, mu_ref, wm_ref, w1_ref, w2_ref, a1_ref, a2_ref,
                    vec_ref, e_ref, r_ref, v_ref, kk_ref, k_ref, b_ref, ld_ref, bonus_ref, z_ref,
                    *, grid_rows):
    t = pl.program_id(1)
    is_ctx = t == 0
    g, mod = g_ref[...], mod_ref[...]
    h = _norm_mod(x_ref[...], g, mod)
    h_up = _norm_mod(xp_ref[...], g, mod)
    h_dn = _norm_mod(xn_ref[...], g, mod)
    n, q = ROW_TILE, D // 4
    i = lax.broadcasted_iota(jnp.int32, (n, q), 0)
    grow = (t - 1) * (n // GRID_W) + i // GRID_W
    pos = jnp.where(is_ctx, i, i % GRID_W)
    has_prev = pos > 0
    has_next = pos < jnp.where(is_ctx, n - 1, GRID_W - 1)
    hq = [h[:, j * q:(j + 1) * q] for j in range(4)]

    def prev_of(x):
        return jnp.where(has_prev, pltpu.roll(x, 1, 0), 0.0)

    def next_of(x):
        return jnp.where(has_next, pltpu.roll(x, n - 1, 0), 0.0)

    up = jnp.concatenate([h_up[:, 2 * q:3 * q], hq[2][:n - GRID_W]], axis=0)
    down = jnp.concatenate([hq[3][GRID_W:], h_dn[:, 3 * q:]], axis=0)
    up = jnp.where(grow > 0, up, 0.0)
    down = jnp.where(grow < grid_rows - 1, down, 0.0)
    shifted = jnp.concatenate([prev_of(hq[0]),
                               jnp.where(is_ctx, prev_of(hq[1]), next_of(hq[1])),
                               jnp.where(is_ctx, next_of(hq[2]), up),
                               jnp.where(is_ctx, next_of(hq[3]), down)], axis=1)
    xx = shifted - h

    hb, xb = h.astype(BF16), xx.astype(BF16)

    def mix(p):
        return hb + xb * mu_ref[p:p + 1, :].astype(BF16)

    vec = vec_ref[...]
    r = jnp.dot(mix(0), wm_ref[0], preferred_element_type=F32)
    k = jnp.dot(mix(2), wm_ref[1], preferred_element_type=F32)
    v = jnp.dot(mix(3), wm_ref[2], preferred_element_type=F32)
    z_ref[...] = jnp.dot(mix(5), wm_ref[3], preferred_element_type=F32).astype(BF16)
    tw = jnp.tanh(jnp.dot(mix(1), w1_ref[...], preferred_element_type=F32))
    ta = jnp.dot(mix(4), a1_ref[...], preferred_element_type=F32)

    kk = k * vec[4:5]
    kk = kk / jnp.maximum(jnp.sqrt(_head_sum(kk * kk, e_ref)), 1e-12)
    r_ref[...] = r
    v_ref[...] = v
    kk_ref[...] = kk
    bonus = jnp.zeros_like(r)
    for d in range(2):
        lw = vec[d:d + 1] + _bdot(tw, w2_ref[d])
        half_decay = -0.5 * math.exp(-0.5)
        ld_ref[d] = half_decay + half_decay * jnp.tanh(0.5 * lw)
        a = 0.5 + 0.5 * jnp.tanh(0.5 * (vec[2 + d:3 + d] + _bdot(ta, a2_ref[d])))
        kd = k * (1.0 + (a - 1.0) * vec[5:6])
        k_ref[d] = kd
        b_ref[d] = kk * a
        bonus = bonus + r * kd * vec[6:7]
    bonus_ref[...] = (_head_sum(bonus, e_ref) * v).astype(BF16)


def _rw_prepare(xc, mod4, g, mu, wm, w1, w2, a1, a2, vec, e, layer, grid_rows):
    n_batch, lc, _ = xc.shape
    tiles = lc // ROW_TILE
    per_tile = ROW_TILE // GRID_W
    last = lc // GRID_W - 1

    def full(a):
        return pl.BlockSpec(a.shape, lambda b, t: (0,) * a.ndim)

    tile_spec = pl.BlockSpec((None, ROW_TILE, D), lambda b, t: (b, t, 0))
    dir_spec = pl.BlockSpec((2, None, ROW_TILE, D), lambda b, t: (0, b, t, 0))
    one = jax.ShapeDtypeStruct((n_batch, lc, D), F32)
    two = jax.ShapeDtypeStruct((2, n_batch, lc, D), F32)
    half = jax.ShapeDtypeStruct((n_batch, lc, D), BF16)
    return pl.pallas_call(
        functools.partial(_rw_prep_kernel, grid_rows=grid_rows),
        grid=(n_batch, tiles),
        in_specs=[pl.BlockSpec((None, GRID_W, D), lambda b, t: (b, jnp.maximum(t * per_tile - 1, 0), 0)),
                  tile_spec,
                  pl.BlockSpec((None, GRID_W, D), lambda b, t: (b, jnp.minimum((t + 1) * per_tile, last), 0)),
                  _mod_spec(layer, n_batch, _identity), full(g), full(mu), full(wm), full(w1), full(w2),
                  full(a1), full(a2), full(vec), full(e)],
        out_specs=[tile_spec, tile_spec, tile_spec, dir_spec, dir_spec, dir_spec, tile_spec, tile_spec],
        out_shape=[one, one, one, two, two, two, half, half],
        compiler_params=_params("parallel", "parallel"),
    )(xc, xc, xc, mod4, g, mu, wm, w1, w2, a1, a2, vec, e)


def _wkv_masks(reverse):
    c, hg = WKV_CHUNK, WKV_GROUP
    t = lax.broadcasted_iota(jnp.int32, (c, hg * c), 0)
    s = lax.broadcasted_iota(jnp.int32, (c, hg * c), 1) % c
    strict = jnp.where((s > t) if reverse else (s < t), 1.0, 0.0)
    incl = jnp.where((s >= t) if reverse else (s <= t), 1.0, 0.0)
    eye = jnp.where(s == t, 1.0, 0.0)
    return strict, jnp.concatenate([incl, incl], axis=1), eye, jnp.concatenate([incl[:, :c]] * 2, axis=1).astype(BF16)


def _wkv_stages(chains, head_lanes, bd_state):
    c, w, hg = WKV_CHUNK, WKV_LANES, WKV_GROUP
    n = range(len(chains))
    refs, rows, masks, reverse = zip(*[(ch["refs"], (ch["rows"], ch["lanes"]), ch["masks"], ch["reverse"])
                                       for ch in chains])
    strict, incl2, eye, cum_mat = zip(*masks)

    def dot(a, b, dims=None):
        if dims is None:
            return jnp.dot(a, b, preferred_element_type=F32)
        return lax.dot_general(a, b, dims, preferred_element_type=F32)

    def stack(x):
        x = x.astype(BF16)
        per_tile = LANE_TILE // RW_HEAD
        zero = jnp.zeros((c, LANE_TILE), BF16)
        blocks = []
        for h in range(hg):
            kept = jnp.where(head_lanes[h % per_tile], x[:, (h // per_tile) * LANE_TILE:(h // per_tile + 1) * LANE_TILE], 0)
            blocks.append(jnp.concatenate([kept if t == h // per_tile else zero for t in range(w // LANE_TILE)], axis=1))
        return jnp.concatenate(blocks, axis=0)

    ld = [refs[i][5][rows[i]] for i in n]
    cum = [dot(cum_mat[i], jnp.concatenate(_split2(ld[i]), axis=0)) for i in n]
    yield
    tot = [cum[i][0:1] if reverse[i] else cum[i][c - 1:c] for i in n]
    e_out = [jnp.exp(-cum[i]) for i in n]
    vb = [refs[i][4][rows[i]].astype(BF16) for i in n]
    rt = [(refs[i][0][rows[i]] * jnp.exp(cum[i])).astype(BF16) for i in n]
    at = [(-refs[i][1][rows[i]] * jnp.exp(cum[i] - ld[i])).astype(BF16) for i in n]
    kt = [(refs[i][3][rows[i]] * e_out[i]).astype(BF16) for i in n]
    bt = [(refs[i][2][rows[i]] * e_out[i]).astype(BF16) for i in n]

    sc = [dot(jnp.concatenate([at[i], rt[i]], axis=0), jnp.concatenate([stack(kt[i]), stack(bt[i])], axis=0), _NT)
          for i in n]
    yield
    a_ak = [(sc[i][:c, :w] * strict[i]).astype(BF16) for i in n]
    p = [sc[i][:c, w:] * strict[i] for i in n]
    a_r = [(sc[i][c:] * incl2[i]).astype(BF16) for i in n]
    tinv = [eye[i] + p[i] for i in n]
    for _ in range(c.bit_length() - 2):
        p = [dot(p[i].astype(BF16), stack(p[i])) for i in n]
        yield
        tinv = [tinv[i] + dot(p[i].astype(BF16), stack(tinv[i])) for i in n]
        yield
    sv = [stack(vb[i]) for i in n]
    av = [dot(jnp.concatenate([a_ak[i], a_r[i][:, :w]], axis=0), sv[i]) for i in n]
    yield

    ht = [ch["ht"][ch["slot"]] for ch in chains]
    wrh = [dot(jnp.concatenate([at[i], rt[i]], axis=0), ht[i].astype(BF16), _NT) for i in n]
    yield
    ub = [dot(tinv[i].astype(BF16), stack((wrh[i][:c] + av[i][:c]).astype(BF16))).astype(BF16) for i in n]
    yield
    y = [wrh[i][c:] + av[i][c:] + dot(a_r[i][:, w:], stack(ub[i])) for i in n]
    yield
    e_tot = [jnp.exp(tot[i]) for i in n]
    e_end = [e_out[i] * e_tot[i] for i in n]
    kb = [jnp.concatenate([(refs[i][3][rows[i]] * e_end[i]).astype(BF16),
                           (refs[i][2][rows[i]] * e_end[i]).astype(BF16)], axis=0) for i in n]
    upd = [dot(jnp.concatenate([vb[i], ub[i]], axis=0), kb[i], _TN) for i in n]
    for i, ch in enumerate(chains):
        ch["y"][rows[i]] = y[i].astype(BF16)
        ch["ht"][ch["slot"]] = jnp.where(bd_state, ht[i] * e_tot[i] + upd[i], 0.0)


def _wkv_kernel(rf, kkf, bf, kf, vf, ldf, rb, kkb, bb, kb, vb, ldb, yf_ref, yb_ref, ht_ref):
    @pl.when(pl.program_id(1) == 0)
    def _():
        ht_ref[...] = jnp.zeros_like(ht_ref)

    c, w, hg = WKV_CHUNK, WKV_LANES, WKV_GROUP
    n_chunks = ROW_TILE // c
    lane = lax.broadcasted_iota(jnp.int32, (c, LANE_TILE), 1)
    head_lanes = [lane // RW_HEAD == r for r in range(LANE_TILE // RW_HEAD)]
    bd_state = (lax.broadcasted_iota(jnp.int32, (w, w), 0) // RW_HEAD
                == lax.broadcasted_iota(jnp.int32, (w, w), 1) // RW_HEAD)
    masks_f, masks_b = _wkv_masks(False), _wkv_masks(True)
    groups = D // w

    def chunk(step):
        rows_f = slice(step * c, (step + 1) * c)
        rows_b = slice((n_chunks - 1 - step) * c, (n_chunks - step) * c)
        chains = []
        for g in range(groups):
            lanes = slice(g * w, (g + 1) * w)
            chains.append(dict(refs=(rf, kkf, bf, kf, vf, ldf), rows=rows_f, lanes=lanes, masks=masks_f,
                               reverse=False, ht=ht_ref, slot=g, y=yf_ref))
            chains.append(dict(refs=(rb, kkb, bb, kb, vb, ldb), rows=rows_b, lanes=lanes, masks=masks_b,
                               reverse=True, ht=ht_ref, slot=groups + g, y=yb_ref))
        return _wkv_stages(chains, head_lanes, bd_state)

    chunks = [chunk(step) for step in range(n_chunks)]
    for _ in range(WKV_SETUP_STAGES):
        next(chunks[0])
    for step in range(n_chunks):
        for k in range(WKV_INVERSE_STAGES + WKV_APPLY_STAGES):
            next(chunks[step], None)
            if step + 1 < n_chunks and k in WKV_SETUP_SLOTS:
                next(chunks[step + 1])


def _wkv(r, kk, b, k, v, ld):
    n_batch, lc, _ = r.shape
    tiles = lc // ROW_TILE
    bt = _bwd_tile(tiles)

    def specs(tile_of, direction):
        one = pl.BlockSpec((None, ROW_TILE, D), lambda bb, s: (bb, tile_of(s), 0))
        two = pl.BlockSpec((None, None, ROW_TILE, D), lambda bb, s: (direction, bb, tile_of(s), 0))
        return one, two

    f1, f2 = specs(_identity, 0)
    b1, b2 = specs(bt, 1)
    return pl.pallas_call(
        _wkv_kernel,
        grid=(n_batch, tiles),
        in_specs=[f1, f1, f2, f2, f1, f2, b1, b1, b2, b2, b1, b2],
        out_specs=[f1, b1],
        out_shape=[jax.ShapeDtypeStruct(r.shape, BF16)] * 2,
        scratch_shapes=[pltpu.VMEM((2 * (D // WKV_LANES), WKV_LANES, WKV_LANES), F32)],
        compiler_params=_params("parallel", "arbitrary"),
    )(r, kk, b, k, v, ld, r, kk, b, k, v, ld)


def _rw_out_kernel(yf_ref, yb_ref, bonus_ref, z_ref, x_ref, mod_ref, vec_ref, e_ref, wo_ref, xo_ref):
    y = yf_ref[...].astype(F32) + yb_ref[...].astype(F32)
    inv = 1.0 / RW_HEAD
    yc = y - _head_sum(y, e_ref) * inv
    yn = yc * lax.rsqrt(_head_sum(yc * yc, e_ref) * inv + RW_LNX_EPS)
    yn = yn * vec_ref[0:1] + vec_ref[1:2]
    o = (yn + bonus_ref[...].astype(F32)) * _silu(z_ref[...].astype(F32))
    xo_ref[...] = x_ref[...] + mod_ref[:, 2 * D:] * _bdot(o, wo_ref[...])


def _rw_output(xc, y_f, y_b, bonus, z, mod4, vec, e, w_out, layer):
    n_batch, lc, _ = xc.shape
    tiles = lc // ROW_TILE
    tile_spec = pl.BlockSpec((None, ROW_TILE, D), lambda b, t: (b, t, 0))

    def full(a):
        return pl.BlockSpec(a.shape, lambda b, t: (0,) * a.ndim)

    return pl.pallas_call(
        _rw_out_kernel,
        grid=(n_batch, tiles),
        in_specs=[tile_spec, tile_spec, tile_spec, tile_spec, tile_spec, _mod_spec(layer, n_batch, _identity),
                  full(vec), full(e), full(w_out)],
        out_specs=tile_spec,
        out_shape=jax.ShapeDtypeStruct(xc.shape, F32),
        compiler_params=_params("parallel", "parallel"),
    )(y_f, y_b, bonus, z, xc, mod4, vec, e, w_out)


def _rwkv(xc, mod4, g, mu, w_rkvg, w0, w1, w2, a0, a1, a2, k_k, k_a, r_k, lnx_g, lnx_b, w_out, layer, grid_rows):
    zeros = jnp.zeros((RW_LORA, D), F32)
    w1c = jnp.concatenate([w1[0], w1[1]], axis=1).astype(BF16)
    a1c = jnp.concatenate([a1[0], a1[1]], axis=1).astype(BF16)
    w2p = jnp.stack([jnp.concatenate([w2[0], zeros]), jnp.concatenate([zeros, w2[1]])]).astype(BF16)
    a2p = jnp.stack([jnp.concatenate([a2[0], zeros]), jnp.concatenate([zeros, a2[1]])]).astype(BF16)
    zrow = jnp.zeros((D,), F32)
    vec = jnp.stack([w0[0], w0[1], a0[0], a0[1], k_k, k_a, r_k.reshape(D), zrow])
    lane = jnp.arange(WKV_LANES) // RW_HEAD
    e = (lane[:, None] == lane[None, :]).astype(BF16)
    r, v, kk, k, b, ld, bonus, z = _rw_prepare(xc, mod4, g.reshape(1, D), mu, w_rkvg.astype(BF16), w1c, w2p,
                                               a1c, a2p, vec, e, layer, grid_rows)
    y_f, y_b = _wkv(r, kk, b, k, v, ld)
    vec_o = jnp.stack([lnx_g, lnx_b] + [zrow] * 6)
    return _rw_output(xc, y_f, y_b, bonus, z, mod4, vec_o, e, w_out.astype(BF16), layer)


def kernel(x, c, ctx, c_ctx, ada_w, ada_b, norm_g, final_g, ret_w_in, ret_decay, ret_w_out, gm_w_in, gm_vnorm_g,
           gm_w_s, gm_b_s, gm_w_out, rw_mu, rw_w_rkvg, rw_w0, rw_w1, rw_w2, rw_a0, rw_a1, rw_a2, rw_k_k, rw_k_a,
           rw_r_k, rw_lnx_g, rw_lnx_b, rw_w_out):
    n_batch, seq_len, _ = x.shape
    ctx_len = ctx.shape[1]
    depth = ada_w.shape[0]
    assert ctx_len == ROW_TILE and seq_len % ROW_TILE == 0 and n_batch < 8

    cvec = jnp.zeros((8, D), F32).at[:n_batch].set(c).at[n_batch].set(c_ctx)
    mod4 = _modulation(cvec, ada_w, ada_b).reshape(depth, 8, 1, 3 * D)
    rope = _rope_table(ctx_len, seq_len)
    k_scale = jnp.concatenate([jnp.ones((RET_HEADS * RET_DK,), F32),
                               jnp.full((RET_HEADS * RET_DK,), RET_DK ** -0.5, F32),
                               jnp.ones((2 * RET_HEADS * RET_DV,), F32)])

    xc = None
    for i in range(depth):
        kind, j = i % N_MIXERS, i // N_MIXERS
        if kind == 0:
            w_in = (ret_w_in[j] * k_scale).astype(BF16)
            stream = (ctx, x) if i == 0 else (xc, xc)
            xc = _retention(*stream, mod4, norm_g[i], w_in, rope, ret_w_out[j].astype(BF16), _ret_tables(ret_decay[j]), i,
                            final_g if i == depth - 1 else None)
        elif kind == 1:
            xc = _gmlp(xc, mod4, norm_g[i].reshape(1, D), gm_w_in[j].astype(BF16), gm_vnorm_g[j].reshape(1, GM_WIDTH),
                       gm_w_s[j].astype(BF16), gm_b_s[j].T, gm_w_out[j].astype(BF16), i)
        else:
            xc = _rwkv(xc, mod4, norm_g[i], rw_mu[j], rw_w_rkvg[j], rw_w0[j], rw_w1[j], rw_w2[j], rw_a0[j],
                       rw_a1[j], rw_a2[j], rw_k_k[j], rw_k_a[j], rw_r_k[j], rw_lnx_g[j], rw_lnx_b[j],
                       rw_w_out[j], i, seq_len // GRID_W)
    assert (depth - 1) % N_MIXERS == 0, "the final norm is fused into a retention layer"
    return xc
```

```python
import functools
import math

import jax
import jax.numpy as jnp
from jax import lax
from jax.experimental import pallas as pl
from jax.experimental.pallas import tpu as pltpu

F32 = jnp.float32
BF16 = jnp.bfloat16

D = 1024
GRID_W = 64
NORM_EPS = 1e-6
N_MIXERS = 3
ROW_TILE = 256
RET_HEADS = 4
RET_DK = 256
RET_DV = 512
ROPE_BASE = 10000.0
GM_WIDTH = 2 * D
GM_GROUPS = 8
GM_CHUNK = 128
RW_HEAD = 64
RW_LORA = 64
RW_LNX_EPS = 64e-5
WKV_CHUNK = 64
WKV_LANES = 256
WKV_GROUP = WKV_LANES // RW_HEAD
LANE_TILE = 128
WKV_SETUP_STAGES, WKV_INVERSE_STAGES, WKV_APPLY_STAGES = 2, 10, 5
WKV_SETUP_SLOTS = (10, 12)
VMEM_LIMIT = 56 * 1024 * 1024

_NT = (((1,), (1,)), ((), ()))
_TN = (((0,), (0,)), ((), ()))


def _params(*sem):
    return pltpu.CompilerParams(dimension_semantics=sem, vmem_limit_bytes=VMEM_LIMIT)


def _silu(x):
    return x / (1.0 + jnp.exp(-x))


def _bdot(a, b, dims=None):
    a = a.astype(BF16)
    b = b.astype(BF16)
    if dims is None:
        return jnp.dot(a, b, preferred_element_type=F32)
    return lax.dot_general(a, b, dims, preferred_element_type=F32)


def _split2(x):
    hi = x.astype(BF16)
    return hi, (x - hi.astype(F32)).astype(BF16)


def _norm_mod(x, g, mod):
    ms = jnp.mean(x * x, axis=-1, keepdims=True)
    y = x * lax.rsqrt(ms + NORM_EPS) * g
    return y * (1.0 + mod[:, D:2 * D]) + mod[:, :D]


def _mod_spec(layer, n_batch, tile_of_step):
    def index(b, s):
        t = tile_of_step(s)
        return (layer, jnp.where(t == 0, n_batch, b), 0, 0)
    return pl.BlockSpec((None, None, 1, 3 * D), index)


def _identity(s):
    return s


def _mod_kernel(c_ref, w_ref, b_ref, o_ref):
    s = _silu(c_ref[...])
    o_ref[...] = jnp.dot(s, w_ref[...], preferred_element_type=F32,
                         precision=lax.Precision.HIGHEST) + b_ref[...]


def _modulation(cvec, ada_w, ada_b):
    depth = ada_w.shape[0]
    return pl.pallas_call(
        _mod_kernel,
        grid=(depth, 3),
        in_specs=[pl.BlockSpec((8, D), lambda i, j: (0, 0)),
                  pl.BlockSpec((None, D, D), lambda i, j: (i, 0, j)),
                  pl.BlockSpec((None, 1, D), lambda i, j: (i, 0, j))],
        out_specs=pl.BlockSpec((None, 8, D), lambda i, j: (i, 0, j)),
        out_shape=jax.ShapeDtypeStruct((depth, 8, 3 * D), F32),
        compiler_params=_params("arbitrary", "arbitrary"),
    )(cvec, ada_w, ada_b.reshape(depth, 1, 3 * D))


PROJ_COLS = 512


def _rope_table(ctx_len, seq_len):
    n_freq = RET_DK // 4
    freqs = ROPE_BASE ** (-jnp.arange(n_freq, dtype=F32) / n_freq)
    t = jnp.arange(seq_len, dtype=jnp.int32)
    row, col = (t // GRID_W).astype(F32), (t % GRID_W).astype(F32)
    ang = jnp.concatenate([row[:, None] * freqs, col[:, None] * freqs], axis=-1)
    ang = jnp.concatenate([jnp.zeros((ctx_len, RET_DK // 2), F32), ang], axis=0)
    return jnp.concatenate([jnp.cos(ang), jnp.sin(ang)], axis=-1)


RET_ROW_LANES = 128


def _ret_table_kernel(dl_ref, mask_ref, row_ref, cd_ref):
    c = ROW_TILE
    i = lax.broadcasted_iota(jnp.int32, (c, c), 0).astype(F32)
    j = lax.broadcasted_iota(jnp.int32, (c, c), 1).astype(F32)
    ir = i[:, :RET_ROW_LANES]

    def log_sigmoid(x):
        return jnp.minimum(x, 0.0) - jnp.log(1.0 + jnp.exp(-jnp.abs(x)))

    for h in range(RET_HEADS):
        lg_f = log_sigmoid(jnp.full((c, c), dl_ref[0, h], F32))
        lg_b = log_sigmoid(jnp.full((c, c), dl_ref[1, h], F32))
        fwd = jnp.where(i >= j, jnp.exp(jnp.maximum(i - j, 0.0) * lg_f), 0.0)
        bwd = jnp.where(j >= i, jnp.exp(jnp.maximum(j - i, 0.0) * lg_b), 0.0)
        mask_ref[h] = fwd + bwd
        lf = log_sigmoid(jnp.full((c, RET_ROW_LANES), dl_ref[0, h], F32))
        lb = log_sigmoid(jnp.full((c, RET_ROW_LANES), dl_ref[1, h], F32))
        row_ref[h, 0] = jnp.exp((ir + 1.0) * lf)
        row_ref[h, 1] = jnp.exp((c - 1.0 - ir) * lf)
        row_ref[h, 2] = jnp.exp((c - ir) * lb)
        row_ref[h, 3] = jnp.exp(ir * lb)
        cd_ref[h, 0] = jnp.exp(c * log_sigmoid(jnp.full((8, RET_DV), dl_ref[0, h], F32)))
        cd_ref[h, 1] = jnp.exp(c * log_sigmoid(jnp.full((8, RET_DV), dl_ref[1, h], F32)))


def _ret_tables(decay_logit):
    c = ROW_TILE
    return pl.pallas_call(
        _ret_table_kernel,
        in_specs=[pl.BlockSpec(memory_space=pltpu.SMEM)],
        out_shape=(jax.ShapeDtypeStruct((RET_HEADS, c, c), F32),
                   jax.ShapeDtypeStruct((RET_HEADS, 4, c, RET_ROW_LANES), F32),
                   jax.ShapeDtypeStruct((RET_HEADS, 2, 8, RET_DV), F32)),
    )(decay_logit)


def _lanes(t, width):
    return jnp.concatenate([t] * (width // RET_ROW_LANES), axis=1)


def _ret_heads(q_ref, k_ref, v_ref, k_col=0, v_col=0):
    return [(q_ref[:, h * RET_DK:(h + 1) * RET_DK], k_ref[:, k_col + h * RET_DK:k_col + (h + 1) * RET_DK],
             v_ref[:, v_col + h * RET_DV:v_col + (h + 1) * RET_DV]) for h in range(RET_HEADS)]


def _ret_fwd_kernel(c_ref, x_ref, mod_ref, g_ref, w_ref, cs_ref, mask_ref, row_ref, cd_ref, p_ref, o_ref, s_ref):
    @pl.when(pl.program_id(1) == 0)
    def _():
        s_ref[...] = jnp.zeros_like(s_ref)

    xt = jnp.where(pl.program_id(1) == 0, c_ref[...], x_ref[...])
    h_in = _norm_mod(xt, g_ref[...], mod_ref[...]).astype(BF16)
    half = RET_DK // 2
    k_col = RET_HEADS * RET_DK
    v_col = 2 * k_col
    z_col = v_col + RET_HEADS * RET_DV
    cos = cs_ref[:, :half]
    sin = cs_ref[:, half:]
    for c0 in range(0, v_col, RET_DK):
        acc = jnp.dot(h_in, w_ref[:, c0:c0 + RET_DK], preferred_element_type=F32)
        t1, t2 = acc[:, :half], acc[:, half:]
        p_ref[:, c0:c0 + half] = (t1 * cos - t2 * sin).astype(BF16)
        p_ref[:, c0 + half:c0 + RET_DK] = (t1 * sin + t2 * cos).astype(BF16)
    for c0 in range(v_col, z_col, PROJ_COLS):
        p_ref[:, c0:c0 + PROJ_COLS] = jnp.dot(h_in, w_ref[:, c0:c0 + PROJ_COLS],
                                              preferred_element_type=F32).astype(BF16)

    def gate_cols(h):
        c0 = z_col + h * RET_DV
        acc = jnp.dot(h_in, w_ref[:, c0:c0 + RET_DV], preferred_element_type=F32)
        p_ref[:, c0:c0 + RET_DV] = _silu(acc).astype(BF16)

    qkv = _ret_heads(p_ref, p_ref, p_ref, k_col, v_col)
    sc = {}

    def scores(h):
        q, k, _ = qkv[h]
        sc[h] = lax.dot_general(q, k, _NT, preferred_element_type=F32)

    def finish(h):
        q, k, v = qkv[h]
        s = s_ref[h]
        qs = _bdot(q, s)
        kd = k.astype(F32) * _lanes(row_ref[h, 1], RET_DK)
        s_ref[h] = s * cd_ref[h, 0, 0:1, :] + _bdot(kd, v, _TN)
        o = _bdot(sc[h] * mask_ref[h], v)
        o_ref[:, h * RET_DV:(h + 1) * RET_DV] = (o + qs * _lanes(row_ref[h, 0], RET_DV)).astype(BF16)

    scores(0)
    for h in range(RET_HEADS):
        if h + 1 < RET_HEADS:
            scores(h + 1)
        gate_cols(h)
        finish(h)


def _ret_bwd_kernel(q_ref, k_ref, v_ref, z_ref, op_ref, c_ref, x_ref, mod_ref, wo_ref, row_ref, cd_ref, fg_ref,
                    xo_ref, s_ref, *, final):
    @pl.when(pl.program_id(1) == 0)
    def _():
        s_ref[...] = jnp.zeros_like(s_ref)

    qkv = _ret_heads(q_ref, k_ref, v_ref)
    gz = {}

    def inter(h):
        q, k, v = qkv[h]
        vv = slice(h * RET_DV, (h + 1) * RET_DV)
        s = s_ref[h]
        o = op_ref[:, vv].astype(F32) + _bdot(q, s) * _lanes(row_ref[h, 2], RET_DV)
        kd = k.astype(F32) * _lanes(row_ref[h, 3], RET_DK)
        s_ref[h] = s * cd_ref[h, 1, 0:1, :] + _bdot(kd, v, _TN)
        o = o * lax.rsqrt(jnp.mean(o * o, axis=-1, keepdims=True) + NORM_EPS)
        gz[h] = (o * z_ref[:, vv].astype(F32)).astype(BF16)

    acc = jnp.zeros((ROW_TILE, D), F32)
    inter(0)
    for h in range(RET_HEADS):
        if h + 1 < RET_HEADS:
            inter(h + 1)
        acc = acc + jnp.dot(gz[h], wo_ref[h * RET_DV:(h + 1) * RET_DV, :], preferred_element_type=F32)
    xt = jnp.where(pl.program_id(1) == 0, c_ref[...], x_ref[...])
    xn = xt + mod_ref[:, 2 * D:] * acc
    if final:
        xn = xn * lax.rsqrt(jnp.mean(xn * xn, axis=-1, keepdims=True) + NORM_EPS) * fg_ref[...]
    xo_ref[...] = xn


def _bwd_tile(tiles):
    return lambda s: jnp.where(s == 0, 0, tiles - s)


def _retention(ctx, x, mod4, g, w_in, rope, w_out, tables, layer, final_g):
    n_batch = x.shape[0]
    x_skip = 0 if x is ctx else ctx.shape[1] // ROW_TILE
    tiles = x.shape[1] // ROW_TILE + x_skip
    lc = tiles * ROW_TILE
    width = RET_HEADS * RET_DV
    state = pltpu.VMEM((RET_HEADS, RET_DK, RET_DV), F32)
    mask, row, cd = tables
    mask_spec = pl.BlockSpec(mask.shape, lambda b, s: (0, 0, 0))
    row_spec = pl.BlockSpec(row.shape, lambda b, s: (0, 0, 0, 0))
    cd_spec = pl.BlockSpec(cd.shape, lambda b, s: (0, 0, 0, 0))

    def col_spec(width_, col, tile_of_step):
        return pl.BlockSpec((None, ROW_TILE, width_), lambda b, s: (b, tile_of_step(s), col))

    def stream_specs(tile_of_step):
        return [pl.BlockSpec((None, ROW_TILE, D), lambda b, s: (b, 0, 0)),
                col_spec(D, 0, lambda s: jnp.maximum(tile_of_step(s) - x_skip, 0))]

    n_proj = w_in.shape[1]
    proj, o_part = pl.pallas_call(
        _ret_fwd_kernel,
        grid=(n_batch, tiles),
        in_specs=stream_specs(_identity) + [
                  _mod_spec(layer, n_batch, _identity),
                  pl.BlockSpec((1, D), lambda b, s: (0, 0)), pl.BlockSpec((D, n_proj), lambda b, s: (0, 0)),
                  pl.BlockSpec((ROW_TILE, RET_DK), lambda b, s: (s, 0)), mask_spec, row_spec, cd_spec],
        out_specs=[col_spec(n_proj, 0, _identity), col_spec(width, 0, _identity)],
        out_shape=[jax.ShapeDtypeStruct((n_batch, lc, n_proj), BF16),
                   jax.ShapeDtypeStruct((n_batch, lc, width), BF16)],
        scratch_shapes=[state],
        compiler_params=_params("parallel", "arbitrary"),
    )(ctx, x, mod4, g.reshape(1, D), w_in, rope, mask, row, cd)

    bt = _bwd_tile(tiles)
    final = final_g is not None
    if final:
        ctx_tiles = 1
        out_spec = col_spec(D, 0, lambda s: bt(jnp.maximum(s, 1)) - ctx_tiles)
        out_shape = jax.ShapeDtypeStruct((n_batch, lc - ctx_tiles * ROW_TILE, D), F32)
        fg = final_g.reshape(1, D)
    else:
        out_spec = col_spec(D, 0, bt)
        out_shape = jax.ShapeDtypeStruct((n_batch, lc, D), F32)
        fg = jnp.ones((1, D), F32)
    return pl.pallas_call(
        functools.partial(_ret_bwd_kernel, final=final),
        grid=(n_batch, tiles),
        in_specs=[col_spec(RET_HEADS * RET_DK, 0, bt), col_spec(RET_HEADS * RET_DK, 1, bt),
                  col_spec(width, 1, bt), col_spec(width, 2, bt), col_spec(width, 0, bt)] + stream_specs(bt) + [
                  _mod_spec(layer, n_batch, bt),
                  pl.BlockSpec((width, D), lambda b, s: (0, 0)), row_spec, cd_spec,
                  pl.BlockSpec((1, D), lambda b, s: (0, 0))],
        out_specs=out_spec,
        out_shape=out_shape,
        scratch_shapes=[state],
        compiler_params=_params("parallel", "arbitrary"),
    )(proj, proj, proj, proj, o_part, ctx, x, mod4, w_out, row, cd, fg)


def _gm_kernel(x_ref, mod_ref, g_ref, w_ref, vg_ref, ws_ref, bs_ref, wo_ref, xo_ref, v_ref):
    mod = mod_ref[...]
    h = _norm_mod(x_ref[...], g_ref[...], mod).astype(BF16)
    for c0 in range(0, GM_WIDTH, PROJ_COLS):
        v_ref[:, c0:c0 + PROJ_COLS] = jnp.dot(h, w_ref[:, GM_WIDTH + c0:GM_WIDTH + c0 + PROJ_COLS],
                                              preferred_element_type=F32)
    v = v_ref[...]
    v = v - jnp.mean(v, axis=-1, keepdims=True)
    vn = (v * lax.rsqrt(jnp.mean(v * v, axis=-1, keepdims=True) + NORM_EPS) * vg_ref[...]).astype(BF16)

    c = GM_CHUNK
    gw = GM_WIDTH // GM_GROUPS
    gated = {}

    def group(g):
        cols = slice(g * gw, (g + 1) * gw)
        u = jnp.dot(h, w_ref[:, cols], preferred_element_type=F32)
        z = jnp.dot(h, w_ref[:, 2 * GM_WIDTH + g * gw:2 * GM_WIDTH + (g + 1) * gw], preferred_element_type=F32)
        mixed = jnp.concatenate([jnp.dot(ws_ref[g], vn[ci * c:(ci + 1) * c, cols], preferred_element_type=F32)
                                 for ci in range(ROW_TILE // c)], axis=0)
        mixed = mixed + jnp.concatenate([bs_ref[:, g:g + 1]] * (ROW_TILE // c), axis=0)
        gated[g] = (u * mixed * _silu(z)).astype(BF16)

    acc = jnp.zeros((ROW_TILE, D), F32)
    group(0)
    for g in range(GM_GROUPS):
        if g + 1 < GM_GROUPS:
            group(g + 1)
        acc = acc + jnp.dot(gated[g], wo_ref[g * gw:(g + 1) * gw, :], preferred_element_type=F32)
    xo_ref[...] = x_ref[...] + mod[:, 2 * D:] * acc


def _gmlp(xc, mod4, g, w_in, vnorm_g, w_s, b_s, w_out, layer):
    n_batch, lc, _ = xc.shape
    tiles = lc // ROW_TILE
    tile_spec = pl.BlockSpec((None, ROW_TILE, D), lambda b, t: (b, t, 0))

    def full(a):
        return pl.BlockSpec(a.shape, lambda b, t: (0,) * a.ndim)

    return pl.pallas_call(
        _gm_kernel,
        grid=(n_batch, tiles),
        in_specs=[tile_spec, _mod_spec(layer, n_batch, _identity), full(g), full(w_in), full(vnorm_g), full(w_s),
                  full(b_s), full(w_out)],
        out_specs=tile_spec,
        out_shape=jax.ShapeDtypeStruct(xc.shape, F32),
        scratch_shapes=[pltpu.VMEM((ROW_TILE, GM_WIDTH), F32)],
        compiler_params=_params("parallel", "parallel"),
    )(xc, mod4, g, w_in, vnorm_g, w_s, b_s, w_out)


def _head_sum(x, e_ref):
    outs = []
    e = e_ref[...]
    for g in range(x.shape[-1] // WKV_LANES):
        hi, lo = _split2(x[:, g * WKV_LANES:(g + 1) * WKV_LANES])
        outs.append(jnp.dot(hi, e, preferred_element_type=F32) + jnp.dot(lo, e, preferred_element_type=F32))
    return jnp.concatenate(outs, axis=1)


def _rw_prep_kernel(xp_ref, x_ref, xn_ref, mod_ref, g_ref, mu_ref, wm_ref, w1_ref, w2_ref, a1_ref, a2_ref,
                    vec_ref, e_ref, r_ref, v_ref, kk_ref, k_ref, b_ref, ld_ref, bonus_ref, z_ref,
                    *, grid_rows):
    t = pl.program_id(1)
    is_ctx = t == 0
    g, mod = g_ref[...], mod_ref[...]
    h = _norm_mod(x_ref[...], g, mod)
    h_up = _norm_mod(xp_ref[...], g, mod)
    h_dn = _norm_mod(xn_ref[...], g, mod)
    n, q = ROW_TILE, D // 4
    i = lax.broadcasted_iota(jnp.int32, (n, q), 0)
    grow = (t - 1) * (n // GRID_W) + i // GRID_W
    pos = jnp.where(is_ctx, i, i % GRID_W)
    has_prev = pos > 0
    has_next = pos < jnp.where(is_ctx, n - 1, GRID_W - 1)
    hq = [h[:, j * q:(j + 1) * q] for j in range(4)]

    def prev_of(x):
        return jnp.where(has_prev, pltpu.roll(x, 1, 0), 0.0)

    def next_of(x):
        return jnp.where(has_next, pltpu.roll(x, n - 1, 0), 0.0)

    up = jnp.concatenate([h_up[:, 2 * q:3 * q], hq[2][:n - GRID_W]], axis=0)
    down = jnp.concatenate([hq[3][GRID_W:], h_dn[:, 3 * q:]], axis=0)
    up = jnp.where(grow > 0, up, 0.0)
    down = jnp.where(grow < grid_rows - 1, down, 0.0)
    shifted = jnp.concatenate([prev_of(hq[0]),
                               jnp.where(is_ctx, prev_of(hq[1]), next_of(hq[1])),
                               jnp.where(is_ctx, next_of(hq[2]), up),
                               jnp.where(is_ctx, next_of(hq[3]), down)], axis=1)
    xx = shifted - h

    hb, xb = h.astype(BF16), xx.astype(BF16)

    def mix(p):
        return hb + xb * mu_ref[p:p + 1, :].astype(BF16)

    vec = vec_ref[...]
    r = jnp.dot(mix(0), wm_ref[0], preferred_element_type=F32)
    k = jnp.dot(mix(2), wm_ref[1], preferred_element_type=F32)
    v = jnp.dot(mix(3), wm_ref[2], preferred_element_type=F32)
    z_ref[...] = jnp.dot(mix(5), wm_ref[3], preferred_element_type=F32).astype(BF16)
    tw = jnp.tanh(jnp.dot(mix(1), w1_ref[...], preferred_element_type=F32))
    ta = jnp.dot(mix(4), a1_ref[...], preferred_element_type=F32)

    kk = k * vec[4:5]
    kk = kk / jnp.maximum(jnp.sqrt(_head_sum(kk * kk, e_ref)), 1e-12)
    r_ref[...] = r.astype(BF16)
    v_ref[...] = v.astype(BF16)
    kk_ref[...] = kk.astype(BF16)
    bonus = jnp.zeros_like(r)
    for d in range(2):
        lw = vec[d:d + 1] + _bdot(tw, w2_ref[d])
        half_decay = -0.5 * math.exp(-0.5)
        ld_ref[d] = half_decay + half_decay * jnp.tanh(0.5 * lw)
        a = 0.5 + 0.5 * jnp.tanh(0.5 * (vec[2 + d:3 + d] + _bdot(ta, a2_ref[d])))
        kd = k * (1.0 + (a - 1.0) * vec[5:6])
        k_ref[d] = kd.astype(BF16)
        b_ref[d] = (kk * a).astype(BF16)
        bonus = bonus + r * kd * vec[6:7]
    bonus_ref[...] = (_head_sum(bonus, e_ref) * v).astype(BF16)


def _rw_prepare(xc, mod4, g, mu, wm, w1, w2, a1, a2, vec, e, layer, grid_rows):
    n_batch, lc, _ = xc.shape
    tiles = lc // ROW_TILE
    per_tile = ROW_TILE // GRID_W
    last = lc // GRID_W - 1

    def full(a):
        return pl.BlockSpec(a.shape, lambda b, t: (0,) * a.ndim)

    tile_spec = pl.BlockSpec((None, ROW_TILE, D), lambda b, t: (b, t, 0))
    dir_spec = pl.BlockSpec((2, None, ROW_TILE, D), lambda b, t: (0, b, t, 0))
    two = jax.ShapeDtypeStruct((2, n_batch, lc, D), F32)
    half = jax.ShapeDtypeStruct((n_batch, lc, D), BF16)
    two_half = jax.ShapeDtypeStruct((2, n_batch, lc, D), BF16)
    return pl.pallas_call(
        functools.partial(_rw_prep_kernel, grid_rows=grid_rows),
        grid=(n_batch, tiles),
        in_specs=[pl.BlockSpec((None, GRID_W, D), lambda b, t: (b, jnp.maximum(t * per_tile - 1, 0), 0)),
                  tile_spec,
                  pl.BlockSpec((None, GRID_W, D), lambda b, t: (b, jnp.minimum((t + 1) * per_tile, last), 0)),
                  _mod_spec(layer, n_batch, _identity), full(g), full(mu), full(wm), full(w1), full(w2),
                  full(a1), full(a2), full(vec), full(e)],
        out_specs=[tile_spec, tile_spec, tile_spec, dir_spec, dir_spec, dir_spec, tile_spec, tile_spec],
        out_shape=[half, half, half, two_half, two_half, two, half, half],
        compiler_params=_params("parallel", "parallel"),
    )(xc, xc, xc, mod4, g, mu, wm, w1, w2, a1, a2, vec, e)


def _wkv_masks(reverse):
    c, hg = WKV_CHUNK, WKV_GROUP
    t = lax.broadcasted_iota(jnp.int32, (c, hg * c), 0)
    s = lax.broadcasted_iota(jnp.int32, (c, hg * c), 1) % c
    strict = jnp.where((s > t) if reverse else (s < t), 1.0, 0.0)
    incl = jnp.where((s >= t) if reverse else (s <= t), 1.0, 0.0)
    eye = jnp.where(s == t, 1.0, 0.0)
    return strict, jnp.concatenate([incl, incl], axis=1), eye, jnp.concatenate([incl[:, :c]] * 2, axis=1).astype(BF16)


def _wkv_stages(chains, head_lanes, bd_state):
    c, w, hg = WKV_CHUNK, WKV_LANES, WKV_GROUP
    n = range(len(chains))
    refs, rows, masks, reverse = zip(*[(ch["refs"], (ch["rows"], ch["lanes"]), ch["masks"], ch["reverse"])
                                       for ch in chains])
    strict, incl2, eye, cum_mat = zip(*masks)

    def dot(a, b, dims=None):
        if dims is None:
            return jnp.dot(a, b, preferred_element_type=F32)
        return lax.dot_general(a, b, dims, preferred_element_type=F32)

    def stack(x):
        x = x.astype(BF16)
        per_tile = LANE_TILE // RW_HEAD
        zero = jnp.zeros((c, LANE_TILE), BF16)
        blocks = []
        for h in range(hg):
            kept = jnp.where(head_lanes[h % per_tile], x[:, (h // per_tile) * LANE_TILE:(h // per_tile + 1) * LANE_TILE], 0)
            blocks.append(jnp.concatenate([kept if t == h // per_tile else zero for t in range(w // LANE_TILE)], axis=1))
        return jnp.concatenate(blocks, axis=0)

    ld = [refs[i][5][rows[i]] for i in n]
    cum = [dot(cum_mat[i], jnp.concatenate(_split2(ld[i]), axis=0)) for i in n]
    yield
    tot = [cum[i][0:1] if reverse[i] else cum[i][c - 1:c] for i in n]
    e_out = [jnp.exp(-cum[i]) for i in n]
    vb = [refs[i][4][rows[i]] for i in n]
    rt = [(refs[i][0][rows[i]].astype(F32) * jnp.exp(cum[i])).astype(BF16) for i in n]
    at = [(-refs[i][1][rows[i]].astype(F32) * jnp.exp(cum[i] - ld[i])).astype(BF16) for i in n]
    kt = [(refs[i][3][rows[i]].astype(F32) * e_out[i]).astype(BF16) for i in n]
    bt = [(refs[i][2][rows[i]].astype(F32) * e_out[i]).astype(BF16) for i in n]

    sc = [dot(jnp.concatenate([at[i], rt[i]], axis=0), jnp.concatenate([stack(kt[i]), stack(bt[i])], axis=0), _NT)
          for i in n]
    yield
    a_ak = [(sc[i][:c, :w] * strict[i]).astype(BF16) for i in n]
    p = [sc[i][:c, w:] * strict[i] for i in n]
    a_r = [(sc[i][c:] * incl2[i]).astype(BF16) for i in n]
    tinv = [eye[i] + p[i] for i in n]
    for _ in range(c.bit_length() - 2):
        p = [dot(p[i].astype(BF16), stack(p[i])) for i in n]
        yield
        tinv = [tinv[i] + dot(p[i].astype(BF16), stack(tinv[i])) for i in n]
        yield
    sv = [stack(vb[i]) for i in n]
    av = [dot(jnp.concatenate([a_ak[i], a_r[i][:, :w]], axis=0), sv[i]) for i in n]
    yield

    ht = [ch["ht"][ch["slot"]] for ch in chains]
    wrh = [dot(jnp.concatenate([at[i], rt[i]], axis=0), ht[i].astype(BF16), _NT) for i in n]
    yield
    ub = [dot(tinv[i].astype(BF16), stack((wrh[i][:c] + av[i][:c]).astype(BF16))).astype(BF16) for i in n]
    yield
    y = [wrh[i][c:] + av[i][c:] + dot(a_r[i][:, w:], stack(ub[i])) for i in n]
    yield
    e_tot = [jnp.exp(tot[i]) for i in n]
    e_end = [e_out[i] * e_tot[i] for i in n]
    kb = [jnp.concatenate([(refs[i][3][rows[i]].astype(F32) * e_end[i]).astype(BF16),
                           (refs[i][2][rows[i]].astype(F32) * e_end[i]).astype(BF16)], axis=0) for i in n]
    upd = [dot(jnp.concatenate([vb[i], ub[i]], axis=0), kb[i], _TN) for i in n]
    for i, ch in enumerate(chains):
        ch["y"][rows[i]] = y[i].astype(BF16)
        ch["ht"][ch["slot"]] = jnp.where(bd_state, ht[i] * e_tot[i] + upd[i], 0.0)


def _wkv_kernel(rf, kkf, bf, kf, vf, ldf, rb, kkb, bb, kb, vb, ldb, yf_ref, yb_ref, ht_ref):
    @pl.when(pl.program_id(1) == 0)
    def _():
        ht_ref[...] = jnp.zeros_like(ht_ref)

    c, w, hg = WKV_CHUNK, WKV_LANES, WKV_GROUP
    n_chunks = ROW_TILE // c
    lane = lax.broadcasted_iota(jnp.int32, (c, LANE_TILE), 1)
    head_lanes = [lane // RW_HEAD == r for r in range(LANE_TILE // RW_HEAD)]
    bd_state = (lax.broadcasted_iota(jnp.int32, (w, w), 0) // RW_HEAD
                == lax.broadcasted_iota(jnp.int32, (w, w), 1) // RW_HEAD)
    masks_f, masks_b = _wkv_masks(False), _wkv_masks(True)
    groups = D // w

    def chunk(step):
        rows_f = slice(step * c, (step + 1) * c)
        rows_b = slice((n_chunks - 1 - step) * c, (n_chunks - step) * c)
        chains = []
        for g in range(groups):
            lanes = slice(g * w, (g + 1) * w)
            chains.append(dict(refs=(rf, kkf, bf, kf, vf, ldf), rows=rows_f, lanes=lanes, masks=masks_f,
                               reverse=False, ht=ht_ref, slot=g, y=yf_ref))
            chains.append(dict(refs=(rb, kkb, bb, kb, vb, ldb), rows=rows_b, lanes=lanes, masks=masks_b,
                               reverse=True, ht=ht_ref, slot=groups + g, y=yb_ref))
        return _wkv_stages(chains, head_lanes, bd_state)

    chunks = [chunk(step) for step in range(n_chunks)]
    for _ in range(WKV_SETUP_STAGES):
        next(chunks[0])
    for step in range(n_chunks):
        for k in range(WKV_INVERSE_STAGES + WKV_APPLY_STAGES):
            next(chunks[step], None)
            if step + 1 < n_chunks and k in WKV_SETUP_SLOTS:
                next(chunks[step + 1])


def _wkv(r, kk, b, k, v, ld):
    n_batch, lc, _ = r.shape
    tiles = lc // ROW_TILE
    bt = _bwd_tile(tiles)

    def specs(tile_of, direction):
        one = pl.BlockSpec((None, ROW_TILE, D), lambda bb, s: (bb, tile_of(s), 0))
        two = pl.BlockSpec((None, None, ROW_TILE, D), lambda bb, s: (direction, bb, tile_of(s), 0))
        return one, two

    f1, f2 = specs(_identity, 0)
    b1, b2 = specs(bt, 1)
    return pl.pallas_call(
        _wkv_kernel,
        grid=(n_batch, tiles),
        in_specs=[f1, f1, f2, f2, f1, f2, b1, b1, b2, b2, b1, b2],
        out_specs=[f1, b1],
        out_shape=[jax.ShapeDtypeStruct(r.shape, BF16)] * 2,
        scratch_shapes=[pltpu.VMEM((2 * (D // WKV_LANES), WKV_LANES, WKV_LANES), F32)],
        compiler_params=_params("parallel", "arbitrary"),
    )(r, kk, b, k, v, ld, r, kk, b, k, v, ld)


def _rw_out_kernel(yf_ref, yb_ref, bonus_ref, z_ref, x_ref, mod_ref, vec_ref, e_ref, wo_ref, xo_ref):
    y = yf_ref[...].astype(F32) + yb_ref[...].astype(F32)
    inv = 1.0 / RW_HEAD
    yc = y - _head_sum(y, e_ref) * inv
    yn = yc * lax.rsqrt(_head_sum(yc * yc, e_ref) * inv + RW_LNX_EPS)
    yn = yn * vec_ref[0:1] + vec_ref[1:2]
    o = (yn + bonus_ref[...].astype(F32)) * _silu(z_ref[...].astype(F32))
    xo_ref[...] = x_ref[...] + mod_ref[:, 2 * D:] * _bdot(o, wo_ref[...])


def _rw_output(xc, y_f, y_b, bonus, z, mod4, vec, e, w_out, layer):
    n_batch, lc, _ = xc.shape
    tiles = lc // ROW_TILE
    tile_spec = pl.BlockSpec((None, ROW_TILE, D), lambda b, t: (b, t, 0))

    def full(a):
        return pl.BlockSpec(a.shape, lambda b, t: (0,) * a.ndim)

    return pl.pallas_call(
        _rw_out_kernel,
        grid=(n_batch, tiles),
        in_specs=[tile_spec, tile_spec, tile_spec, tile_spec, tile_spec, _mod_spec(layer, n_batch, _identity),
                  full(vec), full(e), full(w_out)],
        out_specs=tile_spec,
        out_shape=jax.ShapeDtypeStruct(xc.shape, F32),
        compiler_params=_params("parallel", "parallel"),
    )(y_f, y_b, bonus, z, xc, mod4, vec, e, w_out)


def _rwkv(xc, mod4, g, mu, w_rkvg, w0, w1, w2, a0, a1, a2, k_k, k_a, r_k, lnx_g, lnx_b, w_out, layer, grid_rows):
    zeros = jnp.zeros((RW_LORA, D), F32)
    w1c = jnp.concatenate([w1[0], w1[1]], axis=1).astype(BF16)
    a1c = jnp.concatenate([a1[0], a1[1]], axis=1).astype(BF16)
    w2p = jnp.stack([jnp.concatenate([w2[0], zeros]), jnp.concatenate([zeros, w2[1]])]).astype(BF16)
    a2p = jnp.stack([jnp.concatenate([a2[0], zeros]), jnp.concatenate([zeros, a2[1]])]).astype(BF16)
    zrow = jnp.zeros((D,), F32)
    vec = jnp.stack([w0[0], w0[1], a0[0], a0[1], k_k, k_a, r_k.reshape(D), zrow])
    lane = jnp.arange(WKV_LANES) // RW_HEAD
    e = (lane[:, None] == lane[None, :]).astype(BF16)
    r, v, kk, k, b, ld, bonus, z = _rw_prepare(xc, mod4, g.reshape(1, D), mu, w_rkvg.astype(BF16), w1c, w2p,
                                               a1c, a2p, vec, e, layer, grid_rows)
    y_f, y_b = _wkv(r, kk, b, k, v, ld)
    vec_o = jnp.stack([lnx_g, lnx_b] + [zrow] * 6)
    return _rw_output(xc, y_f, y_b, bonus, z, mod4, vec_o, e, w_out.astype(BF16), layer)


def kernel(x, c, ctx, c_ctx, ada_w, ada_b, norm_g, final_g, ret_w_in, ret_decay, ret_w_out, gm_w_in, gm_vnorm_g,
           gm_w_s, gm_b_s, gm_w_out, rw_mu, rw_w_rkvg, rw_w0, rw_w1, rw_w2, rw_a0, rw_a1, rw_a2, rw_k_k, rw_k_a,
           rw_r_k, rw_lnx_g, rw_lnx_b, rw_w_out):
    n_batch, seq_len, _ = x.shape
    ctx_len = ctx.shape[1]
    depth = ada_w.shape[0]
    assert ctx_len == ROW_TILE and seq_len % ROW_TILE == 0 and n_batch < 8

    cvec = jnp.zeros((8, D), F32).at[:n_batch].set(c).at[n_batch].set(c_ctx)
    mod4 = _modulation(cvec, ada_w, ada_b).reshape(depth, 8, 1, 3 * D)
    rope = _rope_table(ctx_len, seq_len)
    k_scale = jnp.concatenate([jnp.ones((RET_HEADS * RET_DK,), F32),
                               jnp.full((RET_HEADS * RET_DK,), RET_DK ** -0.5, F32),
                               jnp.ones((2 * RET_HEADS * RET_DV,), F32)])

    xc = None
    for i in range(depth):
        kind, j = i % N_MIXERS, i // N_MIXERS
        if kind == 0:
            w_in = (ret_w_in[j] * k_scale).astype(BF16)
            stream = (ctx, x) if i == 0 else (xc, xc)
            xc = _retention(*stream, mod4, norm_g[i], w_in, rope, ret_w_out[j].astype(BF16), _ret_tables(ret_decay[j]), i,
                            final_g if i == depth - 1 else None)
        elif kind == 1:
            xc = _gmlp(xc, mod4, norm_g[i].reshape(1, D), gm_w_in[j].astype(BF16), gm_vnorm_g[j].reshape(1, GM_WIDTH),
                       gm_w_s[j].astype(BF16), gm_b_s[j].T, gm_w_out[j].astype(BF16), i)
        else:
            xc = _rwkv(xc, mod4, norm_g[i], rw_mu[j], rw_w_rkvg[j], rw_w0[j], rw_w1[j], rw_w2[j], rw_a0[j],
                       rw_a1[j], rw_a2[j], rw_k_k[j], rw_k_a[j], rw_r_k[j], rw_lnx_g[j], rw_lnx_b[j],
                       rw_w_out[j], i, seq_len // GRID_W)
    assert (depth - 1) % N_MIXERS == 0, "the final norm is fused into a retention layer"
    return xc
```
